```python
import math
import jax, jax.numpy as jnp
from jax import lax
import numpy as np

D_MODEL = 1024
BATCH = 32
SEQ = 2048
DEPTH = 1

HEAD_DIM = 64
SB_HEADS = 8
DIL_GROUPS = ((128, 1), (512, 4), (2048, 16))
DIL_HEADS = 4
MEM_HEADS = 4
MEM_HEAD_DIM = 128
MEM_LEN = 256
N_BRANCHES = 3
D_FF = ((-(-8 * D_MODEL // 3) + 255) // 256) * 256
BLOCK = 128
ROPE_THETA = 10000.0
NORM_EPS = 1e-6
NEG_INF = -1e30

SB_W = SB_HEADS * HEAD_DIM
DIL_W = DIL_HEADS * HEAD_DIM
MEM_W = MEM_HEADS * MEM_HEAD_DIM
IN_SPLITS = (SB_W,) * 3 + (DIL_W,) * (3 * len(DIL_GROUPS)) + (MEM_W,)
D_IN = sum(IN_SPLITS)

kernel_name = 'hybrid_stickbreak_dilated_memory_block'


def rms_norm(x, g):
    xf = x.astype(jnp.float32)
    y = xf * lax.rsqrt(jnp.mean(xf * xf, axis=-1, keepdims=True) + NORM_EPS)
    return (y * g.astype(jnp.float32)).astype(x.dtype)


def rope(x, pos):
    dh = x.shape[-1]
    half = dh // 2
    inv_freq = ROPE_THETA ** (-jnp.arange(half, dtype=jnp.float32) * 2.0 / dh)
    ang = pos.astype(jnp.float32)[:, None] * inv_freq[None, :]
    cos = jnp.cos(ang)[None, :, None, :]
    sin = jnp.sin(ang)[None, :, None, :]
    xf = x.astype(jnp.float32)
    x1, x2 = xf[..., :half], xf[..., half:]
    return jnp.concatenate([x1 * cos - x2 * sin, x2 * cos + x1 * sin], axis=-1).astype(x.dtype)


def stick_breaking_attention(q, k, v):
    B, S, H, dh = q.shape
    scale = dh ** -0.5
    outs = []
    for i in range(S // BLOCK):
        t0 = i * BLOCK
        t1 = t0 + BLOCK
        z = jnp.einsum('bqhd,bkhd->bhqk', q[:, t0:t1], k[:, :t1]).astype(jnp.float32) * scale
        t_pos = t0 + jnp.arange(BLOCK)[:, None]
        s_pos = jnp.arange(t1)[None, :]
        causal = s_pos < t_pos
        log_beta = jax.nn.log_sigmoid(z)
        log_keep = jnp.where(causal, jax.nn.log_sigmoid(-z), 0.0)
        log_keep_after = lax.cumsum(log_keep, axis=3, reverse=True) - log_keep
        weight = jnp.where(causal, jnp.exp(log_beta + log_keep_after), 0.0)
        outs.append(jnp.einsum('bhqk,bkhd->bqhd', weight, v[:, :t1].astype(jnp.float32)))
    return jnp.concatenate(outs, axis=1).astype(q.dtype)


def banded_attention(q, k, v, span):
    N, L, H, dh = q.shape
    nb = -(-L // BLOCK)
    pad = nb * BLOCK - L
    padf = lambda t: jnp.pad(t, ((0, 0), (0, pad), (0, 0), (0, 0))).reshape(N, nb, BLOCK, H, dh)
    qb, kb, vb = padf(q), padf(k), padf(v)
    prev = lambda t: jnp.concatenate([jnp.zeros_like(t[:, :1]), t[:, :-1]], axis=1)
    kk = jnp.concatenate([prev(kb), kb], axis=2)
    vv = jnp.concatenate([prev(vb), vb], axis=2)
    s = jnp.einsum('nbqhd,nbkhd->nbhqk', qb, kk).astype(jnp.float32) * (dh ** -0.5)
    qi = jnp.arange(BLOCK)[:, None] + BLOCK
    kj = jnp.arange(2 * BLOCK)[None, :]
    dist = qi - kj
    band = (dist >= 0) & (dist <= span)
    has_prev = (jnp.arange(nb)[:, None, None] > 0) | (kj[None] >= BLOCK)
    valid = band[None] & has_prev
    s = jnp.where(valid[None, :, None], s, NEG_INF)
    m = jnp.max(s, axis=-1, keepdims=True)
    p = jnp.exp(s - m)
    den = jnp.sum(p, axis=-1, keepdims=True)
    o = jnp.einsum('nbhqk,nbkhd->nbhqd', p, vv.astype(jnp.float32)) / den
    lse = (m + jnp.log(den))[..., 0]
    o = o.transpose(0, 1, 3, 2, 4).reshape(N, nb * BLOCK, H, dh)[:, :L]
    lse = lse.transpose(0, 1, 3, 2).reshape(N, nb * BLOCK, H)[:, :L]
    return o, lse


def dilated_window_attention(q, k, v, window, dilation):
    B, S, H, dh = q.shape
    L = S // dilation
    def to_sub(t):
        return t.reshape(B, L, dilation, H, dh).transpose(0, 2, 1, 3, 4).reshape(B * dilation, L, H, dh)
    o, lse = banded_attention(to_sub(q), to_sub(k), to_sub(v), window // dilation)
    o = o.reshape(B, dilation, L, H, dh).transpose(0, 2, 1, 3, 4).reshape(B, S, H, dh)
    lse = lse.reshape(B, dilation, L, H).transpose(0, 2, 1, 3).reshape(B, S, H)
    return o, lse


def memory_cross_attention(q, k, v):
    s = jnp.einsum('bshd,bmhd->bhsm', q, k).astype(jnp.float32) * (q.shape[-1] ** -0.5)
    p = jax.nn.softmax(s, axis=-1)
    return jnp.einsum('bhsm,bmhd->bshd', p, v.astype(jnp.float32)).astype(q.dtype)


def setup_inputs(seed: int = 0) -> dict:
    key = jax.random.key(seed)
    ks = jax.random.split(key, 20)
    def w(k, shape, fan_in):
        return jax.random.normal(k, shape, jnp.float32) * (fan_in ** -0.5)
    def gain(k):
        return 1.0 + 0.05 * jax.random.normal(k, (DEPTH, D_MODEL), jnp.float32)
    return {
        'x': jax.random.normal(ks[0], (BATCH, SEQ, D_MODEL), jnp.float32),
        'mem': jax.random.normal(ks[1], (BATCH, MEM_LEN, D_MODEL), jnp.float32),
        'g_pre_mix': gain(ks[2]),
        'g_post_mix': gain(ks[3]),
        'g_pre_ffn': gain(ks[4]),
        'g_post_ffn': gain(ks[5]),
        'g_mem': gain(ks[6]),
        'w_in': w(ks[7], (DEPTH, D_MODEL, D_IN), D_MODEL),
        'w_mem_kv': w(ks[8], (DEPTH, D_MODEL, 2 * MEM_W), D_MODEL),
        'w_br_sb': w(ks[9], (DEPTH, SB_W, D_MODEL), SB_W),
        'w_br_dil': w(ks[10], (DEPTH, DIL_W, D_MODEL), DIL_W),
        'w_br_mem': w(ks[11], (DEPTH, MEM_W, D_MODEL), MEM_W),
        'w_gate': w(ks[12], (DEPTH, D_MODEL, N_BRANCHES * D_MODEL), D_MODEL),
        'b_gate': 0.02 * jax.random.normal(ks[13], (DEPTH, N_BRANCHES * D_MODEL), jnp.float32),
        'w_o': w(ks[14], (DEPTH, D_MODEL, D_MODEL), D_MODEL),
        'w_ffn_in': w(ks[15], (DEPTH, D_MODEL, 2 * D_FF), D_MODEL),
        'w_ffn_out': w(ks[16], (DEPTH, D_FF, D_MODEL), D_FF),
    }


def reference(x, mem, g_pre_mix, g_post_mix, g_pre_ffn, g_post_ffn, g_mem, w_in, w_mem_kv,
              w_br_sb, w_br_dil, w_br_mem, w_gate, b_gate, w_o, w_ffn_in, w_ffn_out):
    B, S, D = x.shape
    pos = jnp.arange(S)
    split_idx = [int(i) for i in np.cumsum(IN_SPLITS)[:-1]]
    n_g = len(DIL_GROUPS)
    for l in range(DEPTH):
        h = rms_norm(x, g_pre_mix[l])
        proj = jnp.einsum('bsd,de->bse', h, w_in[l])
        parts = jnp.split(proj, split_idx, axis=-1)
        heads = lambda t, n, dh: t.reshape(B, S, n, dh)

        q_a, k_a, v_a = (heads(t, SB_HEADS, HEAD_DIM) for t in parts[0:3])
        o_a = stick_breaking_attention(q_a, k_a, v_a).reshape(B, S, SB_W)

        outs, lses = [], []
        for g, (window, dilation) in enumerate(DIL_GROUPS):
            q_g, k_g, v_g = (heads(t, DIL_HEADS, HEAD_DIM) for t in parts[3 + 3 * g: 6 + 3 * g])
            o_g, lse_g = dilated_window_attention(rope(q_g, pos), rope(k_g, pos), v_g, window, dilation)
            outs.append(o_g)
            lses.append(lse_g)
        alpha = jax.nn.softmax(jnp.stack(lses, axis=0), axis=0)[..., None]
        o_b = jnp.sum(alpha * jnp.stack(outs, axis=0), axis=0).astype(x.dtype).reshape(B, S, DIL_W)

        q_c = heads(parts[3 + 3 * n_g], MEM_HEADS, MEM_HEAD_DIM)
        kv_m = jnp.einsum('bmd,de->bme', rms_norm(mem, g_mem[l]), w_mem_kv[l])
        k_m = kv_m[..., :MEM_W].reshape(B, MEM_LEN, MEM_HEADS, MEM_HEAD_DIM)
        v_m = kv_m[..., MEM_W:].reshape(B, MEM_LEN, MEM_HEADS, MEM_HEAD_DIM)
        o_c = memory_cross_attention(q_c, k_m, v_m).reshape(B, S, MEM_W)

        y_a = jnp.einsum('bse,ed->bsd', o_a, w_br_sb[l])
        y_b = jnp.einsum('bse,ed->bsd', o_b, w_br_dil[l])
        y_c = jnp.einsum('bse,ed->bsd', o_c, w_br_mem[l])
        gates = jax.nn.sigmoid(jnp.einsum('bsd,de->bse', h, w_gate[l]) + b_gate[l]).reshape(B, S, N_BRANCHES, D)
        merged = gates[:, :, 0] * y_a + gates[:, :, 1] * y_b + gates[:, :, 2] * y_c
        mix = jnp.einsum('bsd,de->bse', merged, w_o[l])
        x = x + rms_norm(mix, g_post_mix[l])

        h2 = rms_norm(x, g_pre_ffn[l])
        gu = jnp.einsum('bsd,df->bsf', h2, w_ffn_in[l])
        f = jax.nn.silu(gu[..., :D_FF]) * gu[..., D_FF:]
        f = jnp.einsum('bsf,fd->bsd', f, w_ffn_out[l])
        x = x + rms_norm(f, g_post_ffn[l])
    return x
```

```python
import functools

import numpy as np
import jax
import jax.numpy as jnp
from jax import lax
from jax.experimental import pallas as pl
from jax.experimental.pallas import tpu as pltpu

HEAD_DIM = 64
SB_HEADS = 8
DIL_GROUPS = ((128, 1), (512, 4), (2048, 16))
DIL_HEADS = 4
MEM_HEADS = 4
MEM_HEAD_DIM = 128
N_BRANCHES = 3
BLOCK = 128
ROPE_THETA = 10000.0
NORM_EPS = 1e-6
NEG_INF = -1e30

SB_W = SB_HEADS * HEAD_DIM
DIL_W = DIL_HEADS * HEAD_DIM
MEM_W = MEM_HEADS * MEM_HEAD_DIM
N_GROUPS = len(DIL_GROUPS)

LANES = 128
TOKEN_TILE = 512
VMEM_LIMIT_BYTES = 56 * 1024 * 1024

BF16 = jnp.bfloat16
F32 = jnp.float32


def _dot(a, b):
    return jnp.dot(a, b, preferred_element_type=F32)


def _dot_nt(a, b):
    return lax.dot_general(a, b, (((1,), (1,)), ((), ())), preferred_element_type=F32)


def _rms_norm(x, g):
    return x * lax.rsqrt(jnp.mean(x * x, axis=-1, keepdims=True) + NORM_EPS) * g


def _const_spec(shape):
    return pl.BlockSpec(shape, lambda *_: (0,) * len(shape), pipeline_mode=pl.Buffered(1))


def _params(*semantics):
    return pltpu.CompilerParams(dimension_semantics=semantics, vmem_limit_bytes=VMEM_LIMIT_BYTES)


def _memkv_kernel(mem_ref, g_ref, w_ref, out_ref):
    h = _rms_norm(mem_ref[0], g_ref[...]).astype(BF16)
    out_ref[0] = _dot(h, w_ref[...]).astype(BF16)


def _memkv(mem, g_mem, w_kv):
    B, M, D = mem.shape
    return pl.pallas_call(
        _memkv_kernel,
        grid=(B,),
        in_specs=[pl.BlockSpec((1, M, D), lambda b: (b, 0, 0)),
                  _const_spec((1, D)),
                  _const_spec(w_kv.shape)],
        out_specs=pl.BlockSpec((1, M, 2 * MEM_W), lambda b: (b, 0, 0)),
        out_shape=jax.ShapeDtypeStruct((B, M, 2 * MEM_W), BF16),
        compiler_params=_params("parallel"),
        name="memkv",
    )(mem, g_mem, w_kv)


def _inproj_kernel(x_ref, g_ref, cos_ref, sin_ref, w_sb_ref, w_qc_ref, w_d0_ref, w_d1_ref, w_d2_ref,
                   sb_ref, qc_ref, d0_ref, d1_ref, d2_ref, split_ref):
    h = _rms_norm(x_ref[0], g_ref[...]).astype(BF16)
    sb_ref[0] = _dot(h, w_sb_ref[...]).astype(BF16)
    qc_ref[0] = (_dot(h, w_qc_ref[...]) * (MEM_HEAD_DIM ** -0.5)).astype(BF16)
    cos = cos_ref[...]
    sin = sin_ref[...]
    tm = h.shape[0]

    def rot(t):
        t1, t2 = t[:, :LANES], t[:, LANES:]
        return jnp.concatenate([t1 * cos - t2 * sin, t2 * cos + t1 * sin], axis=-1)

    for (_, dil), w_ref, out_ref in zip(DIL_GROUPS, (w_d0_ref, w_d1_ref, w_d2_ref),
                                        (d0_ref, d1_ref, d2_ref)):
        p = _dot(h, w_ref[...])
        p = jnp.concatenate([rot(p[:, :DIL_W]), rot(p[:, DIL_W:2 * DIL_W]), p[:, 2 * DIL_W:]], axis=-1)
        if dil == 1:
            out_ref[0, 0] = p.astype(BF16)
        else:
            nchunk = p.shape[1] // LANES
            for c in range(nchunk):
                split_ref[c] = p[:, c * LANES:(c + 1) * LANES]
            for r in range(dil):
                rows = pl.ds(r, tm // dil, stride=dil)
                out_ref[0, r] = jnp.concatenate([split_ref[c, rows, :] for c in range(nchunk)],
                                                axis=-1).astype(BF16)


def _inproj(x, g, cos, sin, w_sb, w_qc, w_dil):
    B, S, D = x.shape
    tm = TOKEN_TILE
    nt = S // tm
    tile = lambda w: pl.BlockSpec((1, tm, w), lambda b, t: (b, t, 0))
    dil_specs = [pl.BlockSpec((1, d, tm // d, 3 * DIL_W), lambda b, t: (b, 0, t, 0)) for _, d in DIL_GROUPS]
    dil_shapes = [jax.ShapeDtypeStruct((B, d, S // d, 3 * DIL_W), BF16) for _, d in DIL_GROUPS]
    return pl.pallas_call(
        _inproj_kernel,
        grid=(B, nt),
        in_specs=[tile(D), _const_spec((1, D)),
                  pl.BlockSpec((tm, LANES), lambda b, t: (t, 0)),
                  pl.BlockSpec((tm, LANES), lambda b, t: (t, 0)),
                  _const_spec(w_sb.shape), _const_spec(w_qc.shape)] + [_const_spec(w.shape) for w in w_dil],
        out_specs=[tile(3 * SB_W), tile(MEM_W)] + dil_specs,
        out_shape=[jax.ShapeDtypeStruct((B, S, 3 * SB_W), BF16),
                   jax.ShapeDtypeStruct((B, S, MEM_W), BF16)] + dil_shapes,
        scratch_shapes=[pltpu.VMEM((3 * DIL_W // LANES, tm, LANES), F32)],
        compiler_params=_params("parallel", "parallel"),
        name="inproj",
    )(x, g, cos, sin, w_sb, w_qc, *w_dil)


def _stick_kernel(q_ref, k_ref, v_ref, o_ref, carry_ref, acc_ref):
    nblk = q_ref.shape[1] // BLOCK
    lane = lax.broadcasted_iota(jnp.int32, (BLOCK, LANES), 1)
    row = lax.broadcasted_iota(jnp.int32, (BLOCK, LANES), 0)
    first_head = lane < HEAD_DIM
    tri = jnp.concatenate([(row > lane), jnp.ones((BLOCK, LANES), jnp.bool_)], axis=1)
    tri = jnp.where(tri, 1.0, 0.0).astype(BF16)
    causal = jnp.concatenate([lane < row, lane < row], axis=0)
    zero = jnp.zeros((), BF16)

    def block(q0, q1, j, diagonal):
        k = k_ref[0, pl.ds(j * BLOCK, BLOCK), :]
        v = v_ref[0, pl.ds(j * BLOCK, BLOCK), :]
        z = jnp.concatenate([_dot_nt(q0, k), _dot_nt(q1, k)], axis=0)
        log_beta = jnp.minimum(z, 0.0) - jnp.log(1.0 + jnp.exp(-jnp.abs(z)))
        log_keep = log_beta - z
        if diagonal:
            log_keep = jnp.where(causal, log_keep, 0.0)
        hi = log_keep.astype(BF16)
        lo = (log_keep - hi.astype(F32)).astype(BF16)
        sums = _dot(hi, tri) + _dot(lo, tri)
        if diagonal:
            keep_after = sums[:, :LANES]
            carry_ref[...] = sums[:, LANES:]
        else:
            carry = carry_ref[...]
            keep_after = sums[:, :LANES] + carry
            carry_ref[...] = carry + sums[:, LANES:]
        weight = jnp.exp(log_beta + keep_after)
        if diagonal:
            weight = jnp.where(causal, weight, 0.0)
        weight = weight.astype(BF16)
        p = jnp.concatenate([weight[:BLOCK], weight[BLOCK:]], axis=1)
        v2 = jnp.concatenate([jnp.where(first_head, v, zero), jnp.where(first_head, zero, v)], axis=0)
        out = _dot(p, v2)
        if diagonal:
            acc_ref[...] = out
        else:
            acc_ref[...] += out

    def q_block(i, _):
        q = q_ref[0, pl.ds(i * BLOCK, BLOCK), :]
        q0 = jnp.where(first_head, q, zero)
        q1 = jnp.where(first_head, zero, q)
        block(q0, q1, i, True)

        def kv_step(t, _):
            block(q0, q1, i - 1 - t, False)
            return 0

        lax.fori_loop(0, i, kv_step, 0)
        o_ref[0, pl.ds(i * BLOCK, BLOCK), :] = acc_ref[...].astype(o_ref.dtype)
        return 0

    lax.fori_loop(0, nblk, q_block, 0)


def _stick(sb):
    B, S, _ = sb.shape
    pairs = SB_W // LANES
    spec = lambda off: pl.BlockSpec((1, S, LANES), lambda b, p: (b, 0, off + p))
    return pl.pallas_call(
        _stick_kernel,
        grid=(B, pairs),
        in_specs=[spec(0), spec(pairs), spec(2 * pairs)],
        out_specs=spec(0),
        out_shape=jax.ShapeDtypeStruct((B, S, SB_W), BF16),
        scratch_shapes=[pltpu.VMEM((2 * BLOCK, LANES), F32), pltpu.VMEM((BLOCK, LANES), F32)],
        compiler_params=_params("parallel", "parallel"),
        name="stick",
    )(sb, sb, sb)


def _dilated_kernel(d0_ref, d1_ref, d2_ref, o_ref, og_ref, lse_ref):
    S = o_ref.shape[1]
    lane_qk = lax.broadcasted_iota(jnp.int32, (BLOCK, DIL_W), 1)
    qk_head = (lane_qk % LANES) // (HEAD_DIM // 2)
    v_head = lane_qk // HEAD_DIM
    zero = jnp.zeros((), BF16)

    def attend(g, ref, dil, span, r, i, has_prev):
        nk = 2 * BLOCK if has_prev else BLOCK
        k0 = (i - 1) * BLOCK if has_prev else 0
        q = ref[0, r, pl.ds(i * BLOCK, BLOCK), 0:DIL_W]
        kk = ref[0, r, pl.ds(k0, nk), DIL_W:2 * DIL_W]
        vv = ref[0, r, pl.ds(k0, nk), 2 * DIL_W:3 * DIL_W]
        qi = lax.broadcasted_iota(jnp.int32, (BLOCK, nk), 0) + (nk - BLOCK)
        kj = lax.broadcasted_iota(jnp.int32, (BLOCK, nk), 1)
        dist = qi - kj
        valid = (dist >= 0) & (dist <= span)
        out = jnp.zeros((BLOCK, DIL_W), F32)
        lse = jnp.zeros((BLOCK, DIL_W), F32)
        for h in range(DIL_HEADS):
            s = _dot_nt(jnp.where(qk_head == h, q, zero), kk)
            s = jnp.where(valid, s, NEG_INF)
            m = jnp.max(s, axis=-1, keepdims=True)
            p = jnp.exp(s - m)
            den = jnp.sum(p, axis=-1, keepdims=True)
            o_h = _dot(p.astype(BF16), vv) / den
            out = jnp.where(v_head == h, o_h, out)
            lse = jnp.where(v_head == h, m + jnp.log(den), lse)
        rows = pl.ds(r + dil * BLOCK * i, BLOCK, stride=dil) if dil > 1 else pl.ds(i * BLOCK, BLOCK)
        for c in range(DIL_W // LANES):
            og_ref[g, c, rows, :] = out[:, c * LANES:(c + 1) * LANES]
            lse_ref[g, c, rows, :] = lse[:, c * LANES:(c + 1) * LANES]

    for g, ((window, dil), ref) in enumerate(zip(DIL_GROUPS, (d0_ref, d1_ref, d2_ref))):
        span = window // dil
        assert span <= BLOCK
        nb = (S // dil) // BLOCK

        def residue(r, _, g=g, ref=ref, dil=dil, span=span, nb=nb):
            attend(g, ref, dil, span, r, 0, False)
            if nb > 1:
                def q_block(i, _):
                    attend(g, ref, dil, span, r, i, True)
                    return 0
                lax.fori_loop(1, nb, q_block, 0)
            return 0

        if dil == 1:
            residue(0, 0)
        else:
            lax.fori_loop(0, dil, residue, 0)

    chunk = 2 * BLOCK

    def merge(c, _):
        rows = pl.ds(c * chunk, chunk)
        for c in range(DIL_W // LANES):
            lses = [lse_ref[g, c, rows, :] for g in range(N_GROUPS)]
            top = functools.reduce(jnp.maximum, lses)
            ws = [jnp.exp(l - top) for l in lses]
            num = functools.reduce(lambda a, b: a + b, [w * og_ref[g, c, rows, :] for g, w in enumerate(ws)])
            den = functools.reduce(lambda a, b: a + b, ws)
            o_ref[0, rows, c * LANES:(c + 1) * LANES] = (num / den).astype(o_ref.dtype)
        return 0

    lax.fori_loop(0, S // chunk, merge, 0)


def _dilated(dils, S):
    B = dils[0].shape[0]
    return pl.pallas_call(
        _dilated_kernel,
        grid=(B,),
        in_specs=[pl.BlockSpec((1,) + d.shape[1:], lambda b: (b, 0, 0, 0)) for d in dils],
        out_specs=pl.BlockSpec((1, S, DIL_W), lambda b: (b, 0, 0)),
        out_shape=jax.ShapeDtypeStruct((B, S, DIL_W), BF16),
        scratch_shapes=[pltpu.VMEM((N_GROUPS, DIL_W // LANES, S, LANES), F32)] * 2,
        compiler_params=_params("parallel"),
        name="dilated",
    )(*dils)


def _mix_kernel(x_ref, oa_ref, ob_ref, qc_ref, kv_ref, g_pre_ref, g_post_ref, w_gate_ref, b_gate_ref,
                w_a_ref, w_b_ref, w_c_ref, w_o_ref, out_ref):
    x = x_ref[0]
    D = x.shape[-1]
    h = _rms_norm(x, g_pre_ref[...]).astype(BF16)

    heads = []
    for hd in range(MEM_HEADS):
        cols = slice(hd * MEM_HEAD_DIM, (hd + 1) * MEM_HEAD_DIM)
        s = _dot_nt(qc_ref[0, :, cols], kv_ref[0, :, cols])
        p = jnp.exp(s - jnp.max(s, axis=-1, keepdims=True))
        den = jnp.sum(p, axis=-1, keepdims=True)
        v = kv_ref[0, :, MEM_W + hd * MEM_HEAD_DIM:MEM_W + (hd + 1) * MEM_HEAD_DIM]
        heads.append((_dot(p.astype(BF16), v) / den).astype(BF16))
    o_c = jnp.concatenate(heads, axis=-1)

    merged = None
    for br, (o, w_ref) in enumerate(((oa_ref[0], w_a_ref), (ob_ref[0], w_b_ref), (o_c, w_c_ref))):
        gate = jax.nn.sigmoid(_dot(h, w_gate_ref[:, br * D:(br + 1) * D]) + b_gate_ref[:, br * D:(br + 1) * D])
        term = gate * _dot(o, w_ref[...])
        merged = term if merged is None else merged + term
    mix = _dot(merged.astype(BF16), w_o_ref[...])
    out_ref[0] = x + _rms_norm(mix, g_post_ref[...])


def _mix(x, o_a, o_b, q_c, kv_m, g_pre, g_post, w_gate, b_gate, w_a, w_b, w_c, w_o):
    B, S, D = x.shape
    tm = TOKEN_TILE
    tile = lambda w: pl.BlockSpec((1, tm, w), lambda b, t: (b, t, 0))
    consts = (g_pre, g_post, w_gate, b_gate, w_a, w_b, w_c, w_o)
    return pl.pallas_call(
        _mix_kernel,
        grid=(B, S // tm),
        in_specs=[tile(D), tile(SB_W), tile(DIL_W), tile(MEM_W),
                  pl.BlockSpec((1,) + kv_m.shape[1:], lambda b, t: (b, 0, 0))]
                 + [_const_spec(c.shape) for c in consts],
        out_specs=tile(D),
        out_shape=jax.ShapeDtypeStruct((B, S, D), F32),
        compiler_params=_params("parallel", "parallel"),
        name="mix",
    )(x, o_a, o_b, q_c, kv_m, *consts)


def _ffn_kernel(x_ref, g_pre_ref, g_post_ref, w_gate_ref, w_up_ref, w_out_ref, out_ref):
    x = x_ref[...]
    h = _rms_norm(x, g_pre_ref[...]).astype(BF16)
    f = jax.nn.silu(_dot(h, w_gate_ref[...])) * _dot(h, w_up_ref[...])
    f = _dot(f.astype(BF16), w_out_ref[...])
    out_ref[...] = x + _rms_norm(f, g_post_ref[...])


def _ffn(x, g_pre, g_post, w_gate, w_up, w_out):
    N, D = x.shape
    tm = TOKEN_TILE
    consts = (g_pre, g_post, w_gate, w_up, w_out)
    return pl.pallas_call(
        _ffn_kernel,
        grid=(N // tm,),
        in_specs=[pl.BlockSpec((tm, D), lambda t: (t, 0))] + [_const_spec(c.shape) for c in consts],
        out_specs=pl.BlockSpec((tm, D), lambda t: (t, 0)),
        out_shape=jax.ShapeDtypeStruct((N, D), F32),
        compiler_params=_params("parallel"),
        name="ffn",
    )(x, *consts)


def _rope_tables(S):
    half = HEAD_DIM // 2
    inv_freq = ROPE_THETA ** (-jnp.arange(half, dtype=F32) * 2.0 / HEAD_DIM)
    ang = jnp.arange(S, dtype=F32)[:, None] * inv_freq[None, :]
    reps = LANES // half
    return jnp.tile(jnp.cos(ang), (1, reps)), jnp.tile(jnp.sin(ang), (1, reps))


def _rotary_layout(w):
    D = w.shape[0]
    w = w.reshape(D, DIL_HEADS, 2, HEAD_DIM // 2)
    return w.transpose(0, 2, 1, 3).reshape(D, DIL_W)


def _split_w_in(w):
    scale = HEAD_DIM ** -0.5
    w_sb = jnp.concatenate([w[:, :SB_W] * scale, w[:, SB_W:3 * SB_W]], axis=1).astype(BF16)
    w_dil = []
    off = 3 * SB_W
    for _ in DIL_GROUPS:
        q, k, v = (w[:, off + i * DIL_W:off + (i + 1) * DIL_W] for i in range(3))
        w_dil.append(jnp.concatenate([_rotary_layout(q) * scale, _rotary_layout(k), v], axis=1).astype(BF16))
        off += 3 * DIL_W
    w_qc = w[:, off:off + MEM_W].astype(BF16)
    return w_sb, w_qc, w_dil


def kernel(x, mem, g_pre_mix, g_post_mix, g_pre_ffn, g_post_ffn, g_mem, w_in, w_mem_kv, w_br_sb, w_br_dil,
           w_br_mem, w_gate, b_gate, w_o, w_ffn_in, w_ffn_out):
    B, S, D = x.shape
    depth = w_in.shape[0]
    d_ff = w_ffn_out.shape[1]
    cos, sin = _rope_tables(S)
    row = lambda v: v.reshape(1, -1)
    for l in range(depth):
        w_sb, w_qc, w_dil = _split_w_in(w_in[l])
        kv_m = _memkv(mem, row(g_mem[l]), w_mem_kv[l].astype(BF16))
        sb, q_c, *dils = _inproj(x, row(g_pre_mix[l]), cos, sin, w_sb, w_qc, w_dil)
        o_a = _stick(sb)
        o_b = _dilated(dils, S)
        x = _mix(x, o_a, o_b, q_c, kv_m, row(g_pre_mix[l]), row(g_post_mix[l]), w_gate[l].astype(BF16),
                 row(b_gate[l]), w_br_sb[l].astype(BF16), w_br_dil[l].astype(BF16), w_br_mem[l].astype(BF16),
                 w_o[l].astype(BF16))
        w_ff = w_ffn_in[l].astype(BF16)
        x = _ffn(x.reshape(B * S, D), row(g_pre_ffn[l]), row(g_post_ffn[l]), w_ff[:, :d_ff], w_ff[:, d_ff:],
                 w_ffn_out[l].astype(BF16)).reshape(B, S, D)
    return x
```

```python
import functools

import numpy as np
import jax
import jax.numpy as jnp
from jax import lax
from jax.experimental import pallas as pl
from jax.experimental.pallas import tpu as pltpu

HEAD_DIM = 64
SB_HEADS = 8
DIL_GROUPS = ((128, 1), (512, 4), (2048, 16))
DIL_HEADS = 4
MEM_HEADS = 4
MEM_HEAD_DIM = 128
N_BRANCHES = 3
BLOCK = 128
ROPE_THETA = 10000.0
NORM_EPS = 1e-6
NEG_INF = -1e30

SB_W = SB_HEADS * HEAD_DIM
DIL_W = DIL_HEADS * HEAD_DIM
MEM_W = MEM_HEADS * MEM_HEAD_DIM
N_GROUPS = len(DIL_GROUPS)

LANES = 128
TOKEN_TILE = 512
VMEM_LIMIT_BYTES = 56 * 1024 * 1024

BF16 = jnp.bfloat16
F32 = jnp.float32


def _dot(a, b):
    return jnp.dot(a, b, preferred_element_type=F32)


def _dot_nt(a, b):
    return lax.dot_general(a, b, (((1,), (1,)), ((), ())), preferred_element_type=F32)


def _rms_norm(x, g):
    return x * lax.rsqrt(jnp.mean(x * x, axis=-1, keepdims=True) + NORM_EPS) * g


def _const_spec(shape):
    return pl.BlockSpec(shape, lambda *_: (0,) * len(shape), pipeline_mode=pl.Buffered(1))


def _params(*semantics):
    return pltpu.CompilerParams(dimension_semantics=semantics, vmem_limit_bytes=VMEM_LIMIT_BYTES)


def _memkv_kernel(mem_ref, g_ref, w_ref, out_ref):
    h = _rms_norm(mem_ref[0], g_ref[...]).astype(BF16)
    out_ref[0] = _dot(h, w_ref[...]).astype(BF16)


def _memkv(mem, g_mem, w_kv):
    B, M, D = mem.shape
    return pl.pallas_call(
        _memkv_kernel,
        grid=(B,),
        in_specs=[pl.BlockSpec((1, M, D), lambda b: (b, 0, 0)),
                  _const_spec((1, D)),
                  _const_spec(w_kv.shape)],
        out_specs=pl.BlockSpec((1, M, 2 * MEM_W), lambda b: (b, 0, 0)),
        out_shape=jax.ShapeDtypeStruct((B, M, 2 * MEM_W), BF16),
        compiler_params=_params("parallel"),
        name="memkv",
    )(mem, g_mem, w_kv)


def _inproj_kernel(x_ref, g_ref, cos_ref, sin_ref, w_sb_ref, w_qc_ref, w_d0_ref, w_d1_ref, w_d2_ref,
                   sb_ref, qc_ref, d0_ref, d1_ref, d2_ref, split_ref):
    h = _rms_norm(x_ref[0], g_ref[...]).astype(BF16)
    sb_ref[0] = _dot(h, w_sb_ref[...]).astype(BF16)
    qc_ref[0] = (_dot(h, w_qc_ref[...]) * (MEM_HEAD_DIM ** -0.5)).astype(BF16)
    cos = cos_ref[...]
    sin = sin_ref[...]
    tm = h.shape[0]

    def rot(t):
        t1, t2 = t[:, :LANES], t[:, LANES:]
        return jnp.concatenate([t1 * cos - t2 * sin, t2 * cos + t1 * sin], axis=-1)

    for (_, dil), w_ref, out_ref in zip(DIL_GROUPS, (w_d0_ref, w_d1_ref, w_d2_ref),
                                        (d0_ref, d1_ref, d2_ref)):
        p = _dot(h, w_ref[...])
        p = jnp.concatenate([rot(p[:, :DIL_W]), rot(p[:, DIL_W:2 * DIL_W]), p[:, 2 * DIL_W:]], axis=-1)
        if dil == 1:
            out_ref[0, 0] = p.astype(BF16)
        else:
            nchunk = p.shape[1] // LANES
            for c in range(nchunk):
                split_ref[c] = p[:, c * LANES:(c + 1) * LANES]
            for r in range(dil):
                rows = pl.ds(r, tm // dil, stride=dil)
                out_ref[0, r] = jnp.concatenate([split_ref[c, rows, :] for c in range(nchunk)],
                                                axis=-1).astype(BF16)


def _inproj(x, g, cos, sin, w_sb, w_qc, w_dil):
    B, S, D = x.shape
    tm = TOKEN_TILE
    nt = S // tm
    tile = lambda w: pl.BlockSpec((1, tm, w), lambda b, t: (b, t, 0))
    dil_specs = [pl.BlockSpec((1, d, tm // d, 3 * DIL_W), lambda b, t: (b, 0, t, 0)) for _, d in DIL_GROUPS]
    dil_shapes = [jax.ShapeDtypeStruct((B, d, S // d, 3 * DIL_W), BF16) for _, d in DIL_GROUPS]
    return pl.pallas_call(
        _inproj_kernel,
        grid=(B, nt),
        in_specs=[tile(D), _const_spec((1, D)),
                  pl.BlockSpec((tm, LANES), lambda b, t: (t, 0)),
                  pl.BlockSpec((tm, LANES), lambda b, t: (t, 0)),
                  _const_spec(w_sb.shape), _const_spec(w_qc.shape)] + [_const_spec(w.shape) for w in w_dil],
        out_specs=[tile(3 * SB_W), tile(MEM_W)] + dil_specs,
        out_shape=[jax.ShapeDtypeStruct((B, S, 3 * SB_W), BF16),
                   jax.ShapeDtypeStruct((B, S, MEM_W), BF16)] + dil_shapes,
        scratch_shapes=[pltpu.VMEM((3 * DIL_W // LANES, tm, LANES), F32)],
        compiler_params=_params("parallel", "parallel"),
        name="inproj",
    )(x, g, cos, sin, w_sb, w_qc, *w_dil)


def _stick_kernel(q_ref, k_ref, v_ref, o_ref, qq_ref, tri_ref, carry_ref, acc_ref):
    nblk = q_ref.shape[1] // BLOCK
    npair = q_ref.shape[2] // LANES
    lane = lax.broadcasted_iota(jnp.int32, (BLOCK, LANES), 1)
    row = lax.broadcasted_iota(jnp.int32, (BLOCK, LANES), 0)
    first_head = lane < HEAD_DIM
    causal = jnp.concatenate([lane < row, lane < row], axis=0)
    zero = jnp.zeros((), BF16)
    tri = jnp.concatenate([(row > lane), jnp.ones((BLOCK, LANES), jnp.bool_)], axis=1)
    tri = jnp.where(tri, 1.0, 0.0).astype(BF16)
    tri_ref[...] = jnp.concatenate([tri, tri], axis=0)

    def block(j, diagonal):
        rows = pl.ds(pl.multiple_of(j * BLOCK, BLOCK), BLOCK)
        pairs = range(npair)
        cols = [slice(p * LANES, (p + 1) * LANES) for p in pairs]
        zs = [_dot_nt(qq_ref[p], k_ref[0, rows, cols[p]]) for p in pairs]
        log_betas, sums = [], []
        for p in pairs:
            z = zs[p]
            log_beta = jnp.minimum(z, 0.0) - jnp.log(1.0 + jnp.exp(-jnp.abs(z)))
            log_keep = log_beta - z
            if diagonal:
                log_keep = jnp.where(causal, log_keep, 0.0)
            hi = log_keep.astype(BF16)
            lo = (log_keep - hi.astype(F32)).astype(BF16)
            log_betas.append(log_beta)
            sums.append(_dot(jnp.concatenate([hi, lo], axis=1), tri_ref[...]))
        for p in pairs:
            if diagonal:
                keep_after = sums[p][:, :LANES]
                carry_ref[p] = sums[p][:, LANES:]
            else:
                carry = carry_ref[p]
                keep_after = sums[p][:, :LANES] + carry
                carry_ref[p] = carry + sums[p][:, LANES:]
            weight = jnp.exp(log_betas[p] + keep_after)
            if diagonal:
                weight = jnp.where(causal, weight, 0.0)
            weight = weight.astype(BF16)
            pw = jnp.concatenate([weight[:BLOCK], weight[BLOCK:]], axis=1)
            v = v_ref[0, rows, cols[p]]
            v2 = jnp.concatenate([jnp.where(first_head, v, zero), jnp.where(first_head, zero, v)], axis=0)
            out = _dot(pw, v2)
            if diagonal:
                acc_ref[p] = out
            else:
                acc_ref[p] += out

    def q_block(i, _):
        rows = pl.ds(pl.multiple_of(i * BLOCK, BLOCK), BLOCK)
        for p in range(npair):
            q = q_ref[0, rows, p * LANES:(p + 1) * LANES]
            qq_ref[p] = jnp.concatenate([jnp.where(first_head, q, zero), jnp.where(first_head, zero, q)], axis=0)
        block(i, True)

        def kv_step(t, _):
            block(i - 1 - t, False)
            return 0

        lax.fori_loop(0, i, kv_step, 0)
        for p in range(npair):
            o_ref[0, rows, p * LANES:(p + 1) * LANES] = acc_ref[p].astype(o_ref.dtype)
        return 0

    lax.fori_loop(0, nblk, q_block, 0)


def _stick(sb):
    B, S, _ = sb.shape
    npair = SB_W // LANES
    spec = lambda c: pl.BlockSpec((1, S, SB_W), lambda b: (b, 0, c))
    return pl.pallas_call(
        _stick_kernel,
        grid=(B,),
        in_specs=[spec(0), spec(1), spec(2)],
        out_specs=spec(0),
        out_shape=jax.ShapeDtypeStruct((B, S, SB_W), BF16),
        scratch_shapes=[pltpu.VMEM((npair, 2 * BLOCK, LANES), BF16),
                        pltpu.VMEM((2 * BLOCK, 2 * LANES), BF16),
                        pltpu.VMEM((npair, 2 * BLOCK, LANES), F32),
                        pltpu.VMEM((npair, BLOCK, LANES), F32)],
        compiler_params=_params("parallel"),
        name="stick",
    )(sb, sb, sb)


def _dilated_kernel(d0_ref, d1_ref, d2_ref, o_ref, og_ref, lse_ref):
    S = o_ref.shape[1]
    nchunk = DIL_W // LANES
    lane_qk = lax.broadcasted_iota(jnp.int32, (BLOCK, DIL_W), 1)
    qk_head = (lane_qk % LANES) // (HEAD_DIM // 2)
    v_head = lane_qk // HEAD_DIM
    zero = jnp.zeros((), BF16)

    def key_window(ref, i):
        L = ref.shape[2]
        nk = min(2 * BLOCK, L)
        k0 = jnp.maximum(i - 1, 0) * BLOCK if nk < L else 0
        return k0, nk

    def scores(ref, r, i):
        k0, nk = key_window(ref, i)
        q = ref[0, r, pl.ds(i * BLOCK, BLOCK), 0:DIL_W]
        kk = ref[0, r, pl.ds(k0, nk), DIL_W:2 * DIL_W]
        qs = jnp.concatenate([jnp.where(qk_head == h, q, zero) for h in range(DIL_HEADS)], axis=0)
        return _dot_nt(qs, kk)

    def attend(g, ref, dil, span, r, i, s):
        k0, nk = key_window(ref, i)
        vv = ref[0, r, pl.ds(k0, nk), 2 * DIL_W:3 * DIL_W]
        qi = lax.broadcasted_iota(jnp.int32, s.shape, 0) % BLOCK + (i * BLOCK - k0)
        kj = lax.broadcasted_iota(jnp.int32, s.shape, 1)
        dist = qi - kj
        s = jnp.where((dist >= 0) & (dist <= span), s, NEG_INF)
        m = jnp.max(s, axis=-1, keepdims=True)
        p = jnp.exp(s - m)
        den = jnp.sum(p, axis=-1, keepdims=True)
        pv = _dot(p.astype(BF16), vv) * (1.0 / den)
        lse_rows = m + jnp.log(den)
        out = pv[:BLOCK]
        lse = jnp.broadcast_to(lse_rows[:BLOCK], (BLOCK, DIL_W))
        for h in range(1, DIL_HEADS):
            out = jnp.where(v_head == h, pv[h * BLOCK:(h + 1) * BLOCK], out)
            lse = jnp.where(v_head == h, lse_rows[h * BLOCK:(h + 1) * BLOCK], lse)
        rows = pl.ds(r + dil * BLOCK * i, BLOCK, stride=dil) if dil > 1 else pl.ds(i * BLOCK, BLOCK)
        for c in range(nchunk):
            og_ref[g, c, rows, :] = out[:, c * LANES:(c + 1) * LANES]
            lse_ref[g, c, rows, :] = lse[:, c * LANES:(c + 1) * LANES]

    side = 2
    for g, ((window, dil), ref) in enumerate(zip(DIL_GROUPS, (d0_ref, d1_ref, d2_ref))):
        span = window // dil
        assert span <= BLOCK
        nb = (S // dil) // BLOCK
        units = dil * nb
        assert units % side == 0

        def unit_group(t, _, g=g, ref=ref, dil=dil, span=span, nb=nb):
            where = [((side * t + u) // nb, (side * t + u) % nb) for u in range(side)]
            ss = [scores(ref, r, i) for r, i in where]
            for (r, i), s in zip(where, ss):
                attend(g, ref, dil, span, r, i, s)
            return 0

        lax.fori_loop(0, units // side, unit_group, 0)

    chunk = 2 * BLOCK

    def merge(t, _):
        rows = pl.ds(t * chunk, chunk)
        for c in range(nchunk):
            lses = [lse_ref[g, c, rows, :] for g in range(N_GROUPS)]
            top = functools.reduce(jnp.maximum, lses)
            ws = [jnp.exp(l - top) for l in lses]
            num = functools.reduce(lambda a, b: a + b, [w * og_ref[g, c, rows, :] for g, w in enumerate(ws)])
            den = functools.reduce(lambda a, b: a + b, ws)
            o_ref[0, rows, c * LANES:(c + 1) * LANES] = (num / den).astype(o_ref.dtype)
        return 0

    lax.fori_loop(0, S // chunk, merge, 0)


def _dilated(dils, S):
    B = dils[0].shape[0]
    return pl.pallas_call(
        _dilated_kernel,
        grid=(B,),
        in_specs=[pl.BlockSpec((1,) + d.shape[1:], lambda b: (b, 0, 0, 0)) for d in dils],
        out_specs=pl.BlockSpec((1, S, DIL_W), lambda b: (b, 0, 0)),
        out_shape=jax.ShapeDtypeStruct((B, S, DIL_W), BF16),
        scratch_shapes=[pltpu.VMEM((N_GROUPS, DIL_W // LANES, S, LANES), F32)] * 2,
        compiler_params=_params("parallel"),
        name="dilated",
    )(*dils)


def _mix_kernel(x_ref, oa_ref, ob_ref, qc_ref, kv_ref, g_pre_ref, g_post_ref, w_gate_ref, b_gate_ref,
                w_a_ref, w_b_ref, w_c_ref, w_o_ref, out_ref):
    x = x_ref[0]
    D = x.shape[-1]
    h = _rms_norm(x, g_pre_ref[...]).astype(BF16)

    heads = []
    for hd in range(MEM_HEADS):
        cols = slice(hd * MEM_HEAD_DIM, (hd + 1) * MEM_HEAD_DIM)
        s = _dot_nt(qc_ref[0, :, cols], kv_ref[0, :, cols])
        p = jnp.exp(s - jnp.max(s, axis=-1, keepdims=True))
        den = jnp.sum(p, axis=-1, keepdims=True)
        v = kv_ref[0, :, MEM_W + hd * MEM_HEAD_DIM:MEM_W + (hd + 1) * MEM_HEAD_DIM]
        heads.append((_dot(p.astype(BF16), v) / den).astype(BF16))
    o_c = jnp.concatenate(heads, axis=-1)

    merged = None
    for br, (o, w_ref) in enumerate(((oa_ref[0], w_a_ref), (ob_ref[0], w_b_ref), (o_c, w_c_ref))):
        gate = jax.nn.sigmoid(_dot(h, w_gate_ref[:, br * D:(br + 1) * D]) + b_gate_ref[:, br * D:(br + 1) * D])
        term = gate * _dot(o, w_ref[...])
        merged = term if merged is None else merged + term
    mix = _dot(merged.astype(BF16), w_o_ref[...])
    out_ref[0] = x + _rms_norm(mix, g_post_ref[...])


def _mix(x, o_a, o_b, q_c, kv_m, g_pre, g_post, w_gate, b_gate, w_a, w_b, w_c, w_o):
    B, S, D = x.shape
    tm = TOKEN_TILE
    tile = lambda w: pl.BlockSpec((1, tm, w), lambda b, t: (b, t, 0))
    consts = (g_pre, g_post, w_gate, b_gate, w_a, w_b, w_c, w_o)
    return pl.pallas_call(
        _mix_kernel,
        grid=(B, S // tm),
        in_specs=[tile(D), tile(SB_W), tile(DIL_W), tile(MEM_W),
                  pl.BlockSpec((1,) + kv_m.shape[1:], lambda b, t: (b, 0, 0))]
                 + [_const_spec(c.shape) for c in consts],
        out_specs=tile(D),
        out_shape=jax.ShapeDtypeStruct((B, S, D), F32),
        compiler_params=_params("parallel", "parallel"),
        name="mix",
    )(x, o_a, o_b, q_c, kv_m, *consts)


def _ffn_kernel(x_ref, g_pre_ref, g_post_ref, w_gate_ref, w_up_ref, w_out_ref, out_ref):
    x = x_ref[...]
    h = _rms_norm(x, g_pre_ref[...]).astype(BF16)
    f = jax.nn.silu(_dot(h, w_gate_ref[...])) * _dot(h, w_up_ref[...])
    f = _dot(f.astype(BF16), w_out_ref[...])
    out_ref[...] = x + _rms_norm(f, g_post_ref[...])


def _ffn(x, g_pre, g_post, w_gate, w_up, w_out):
    N, D = x.shape
    tm = TOKEN_TILE
    consts = (g_pre, g_post, w_gate, w_up, w_out)
    return pl.pallas_call(
        _ffn_kernel,
        grid=(N // tm,),
        in_specs=[pl.BlockSpec((tm, D), lambda t: (t, 0))] + [_const_spec(c.shape) for c in consts],
        out_specs=pl.BlockSpec((tm, D), lambda t: (t, 0)),
        out_shape=jax.ShapeDtypeStruct((N, D), F32),
        compiler_params=_params("parallel"),
        name="ffn",
    )(x, *consts)


def _rope_tables(S):
    half = HEAD_DIM // 2
    inv_freq = ROPE_THETA ** (-jnp.arange(half, dtype=F32) * 2.0 / HEAD_DIM)
    ang = jnp.arange(S, dtype=F32)[:, None] * inv_freq[None, :]
    reps = LANES // half
    return jnp.tile(jnp.cos(ang), (1, reps)), jnp.tile(jnp.sin(ang), (1, reps))


def _rotary_layout(w):
    D = w.shape[0]
    w = w.reshape(D, DIL_HEADS, 2, HEAD_DIM // 2)
    return w.transpose(0, 2, 1, 3).reshape(D, DIL_W)


def _split_w_in(w):
    scale = HEAD_DIM ** -0.5
    w_sb = jnp.concatenate([w[:, :SB_W] * scale, w[:, SB_W:3 * SB_W]], axis=1).astype(BF16)
    w_dil = []
    off = 3 * SB_W
    for _ in DIL_GROUPS:
        q, k, v = (w[:, off + i * DIL_W:off + (i + 1) * DIL_W] for i in range(3))
        w_dil.append(jnp.concatenate([_rotary_layout(q) * scale, _rotary_layout(k), v], axis=1).astype(BF16))
        off += 3 * DIL_W
    w_qc = w[:, off:off + MEM_W].astype(BF16)
    return w_sb, w_qc, w_dil


def kernel(x, mem, g_pre_mix, g_post_mix, g_pre_ffn, g_post_ffn, g_mem, w_in, w_mem_kv, w_br_sb, w_br_dil,
           w_br_mem, w_gate, b_gate, w_o, w_ffn_in, w_ffn_out):
    B, S, D = x.shape
    depth = w_in.shape[0]
    d_ff = w_ffn_out.shape[1]
    cos, sin = _rope_tables(S)
    row = lambda v: v.reshape(1, -1)
    for l in range(depth):
        w_sb, w_qc, w_dil = _split_w_in(w_in[l])
        kv_m = _memkv(mem, row(g_mem[l]), w_mem_kv[l].astype(BF16))
        sb, q_c, *dils = _inproj(x, row(g_pre_mix[l]), cos, sin, w_sb, w_qc, w_dil)
        o_a = _stick(sb)
        o_b = _dilated(dils, S)
        x = _mix(x, o_a, o_b, q_c, kv_m, row(g_pre_mix[l]), row(g_post_mix[l]), w_gate[l].astype(BF16),
                 row(b_gate[l]), w_br_sb[l].astype(BF16), w_br_dil[l].astype(BF16), w_br_mem[l].astype(BF16),
                 w_o[l].astype(BF16))
        w_ff = w_ffn_in[l].astype(BF16)
        x = _ffn(x.reshape(B * S, D), row(g_pre_ffn[l]), row(g_post_ffn[l]), w_ff[:, :d_ff], w_ff[:, d_ff:],
                 w_ffn_out[l].astype(BF16)).reshape(B, S, D)
    return x
```

```python
import functools

import jax
import jax.numpy as jnp
from jax import lax
from jax.experimental import pallas as pl
from jax.experimental.pallas import tpu as pltpu

HEAD_DIM = 64
SB_HEADS = 8
DIL_GROUPS = ((128, 1), (512, 4), (2048, 16))
DIL_HEADS = 4
MEM_HEADS = 4
MEM_HEAD_DIM = 128
N_BRANCHES = 3
BLOCK = 128
ROPE_THETA = 10000.0
NORM_EPS = 1e-6
NEG_INF = -1e30

SB_W = SB_HEADS * HEAD_DIM
DIL_W = DIL_HEADS * HEAD_DIM
MEM_W = MEM_HEADS * MEM_HEAD_DIM
N_GROUPS = len(DIL_GROUPS)

LOG2E = 1.4426950408889634
LN2 = 0.6931471805599453
EXP2_UNDERFLOW = 127.0

LANES = 128
TOKEN_TILE = 1024
SUB_TILE = 512
FFN_TILE = 512
VMEM_LIMIT_BYTES = 56 * 1024 * 1024

BF16 = jnp.bfloat16
F32 = jnp.float32


def _dot(a, b):
    return jnp.dot(a, b, preferred_element_type=F32)


def _dot_nt(a, b):
    return lax.dot_general(a, b, (((1,), (1,)), ((), ())), preferred_element_type=F32)


def _rms_norm(x, g):
    return x * lax.rsqrt(jnp.mean(x * x, axis=-1, keepdims=True) + NORM_EPS) * g


def _const_spec(shape):
    return pl.BlockSpec(shape, lambda *_: (0,) * len(shape), pipeline_mode=pl.Buffered(1))


def _params(*semantics):
    return pltpu.CompilerParams(dimension_semantics=semantics, vmem_limit_bytes=VMEM_LIMIT_BYTES)


def _sub_tiles(tm):
    return [slice(s, s + SUB_TILE) for s in range(0, tm, SUB_TILE)]


def _memkv_kernel(mem_ref, g_ref, w_ref, out_ref):
    h = _rms_norm(mem_ref[0], g_ref[...]).astype(BF16)
    out_ref[0] = _dot(h, w_ref[...]).astype(BF16)


def _memkv(mem, g_mem, w_kv):
    B, M, D = mem.shape
    return pl.pallas_call(
        _memkv_kernel,
        grid=(B,),
        in_specs=[pl.BlockSpec((1, M, D), lambda b: (b, 0, 0)),
                  _const_spec((1, D)),
                  _const_spec(w_kv.shape)],
        out_specs=pl.BlockSpec((1, M, 2 * MEM_W), lambda b: (b, 0, 0)),
        out_shape=jax.ShapeDtypeStruct((B, M, 2 * MEM_W), BF16),
        compiler_params=_params("parallel"),
        name="memkv",
    )(mem, g_mem, w_kv)


def _inproj_kernel(x_ref, g_ref, cos_ref, sin_ref, w_sb_ref, w_qc_ref, w_d0_ref, w_d1_ref, w_d2_ref,
                   sb_ref, qc_ref, d0_ref, d1_ref, d2_ref, split_ref):
    def rot(t, cos, sin):
        t1, t2 = t[:, :LANES], t[:, LANES:]
        return jnp.concatenate([t1 * cos - t2 * sin, t2 * cos + t1 * sin], axis=-1)

    for s, rows in enumerate(_sub_tiles(x_ref.shape[1])):
        h = _rms_norm(x_ref[0, rows, :], g_ref[...]).astype(BF16)
        sb = _dot(h, w_sb_ref[...])
        sb_ref[0, rows, :] = jnp.concatenate([sb[:, :SB_W] * LOG2E, sb[:, SB_W:]], axis=-1).astype(BF16)
        qc_ref[0, rows, :] = (_dot(h, w_qc_ref[...]) * (MEM_HEAD_DIM ** -0.5)).astype(BF16)
        cos = cos_ref[rows, :]
        sin = sin_ref[rows, :]
        for (_, dil), w_ref, out_ref in zip(DIL_GROUPS, (w_d0_ref, w_d1_ref, w_d2_ref),
                                            (d0_ref, d1_ref, d2_ref)):
            p = _dot(h, w_ref[...])
            p = jnp.concatenate([rot(p[:, :DIL_W], cos, sin) * LOG2E, rot(p[:, DIL_W:2 * DIL_W], cos, sin),
                                 p[:, 2 * DIL_W:]], axis=-1)
            if dil == 1:
                out_ref[0, 0, rows, :] = p.astype(BF16)
            else:
                nchunk = p.shape[1] // LANES
                per = SUB_TILE // dil
                for c in range(nchunk):
                    split_ref[s, c] = p[:, c * LANES:(c + 1) * LANES]
                for r in range(dil):
                    strided = pl.ds(r, per, stride=dil)
                    out_ref[0, r, s * per:(s + 1) * per, :] = jnp.concatenate(
                        [split_ref[s, c, strided, :] for c in range(nchunk)], axis=-1).astype(BF16)


def _inproj(x, g, cos, sin, w_sb, w_qc, w_dil):
    B, S, D = x.shape
    tm = TOKEN_TILE
    nt = S // tm
    tile = lambda w: pl.BlockSpec((1, tm, w), lambda b, t: (b, t, 0))
    dil_specs = [pl.BlockSpec((1, d, tm // d, 3 * DIL_W), lambda b, t: (b, 0, t, 0)) for _, d in DIL_GROUPS]
    dil_shapes = [jax.ShapeDtypeStruct((B, d, S // d, 3 * DIL_W), BF16) for _, d in DIL_GROUPS]
    return pl.pallas_call(
        _inproj_kernel,
        grid=(B, nt),
        in_specs=[tile(D), _const_spec((1, D)),
                  pl.BlockSpec((tm, LANES), lambda b, t: (t, 0)),
                  pl.BlockSpec((tm, LANES), lambda b, t: (t, 0)),
                  _const_spec(w_sb.shape), _const_spec(w_qc.shape)] + [_const_spec(w.shape) for w in w_dil],
        out_specs=[tile(3 * SB_W), tile(MEM_W)] + dil_specs,
        out_shape=[jax.ShapeDtypeStruct((B, S, 3 * SB_W), BF16),
                   jax.ShapeDtypeStruct((B, S, MEM_W), BF16)] + dil_shapes,
        scratch_shapes=[pltpu.VMEM((tm // SUB_TILE, 3 * DIL_W // LANES, SUB_TILE, LANES), F32)],
        compiler_params=_params("parallel", "parallel"),
        name="inproj",
    )(x, g, cos, sin, w_sb, w_qc, *w_dil)


def _stick_kernel(q_ref, k_ref, v_ref, o_ref, qq_ref, tri_ref, carry_ref, acc_ref):
    nblk = q_ref.shape[1] // BLOCK
    npair = q_ref.shape[2] // LANES
    pairs = range(npair)
    lane = lax.broadcasted_iota(jnp.int32, (BLOCK, LANES), 1)
    row = lax.broadcasted_iota(jnp.int32, (BLOCK, LANES), 0)
    first_head = lane < HEAD_DIM
    causal = jnp.concatenate([lane < row, lane < row], axis=0)
    zero = jnp.zeros((), BF16)
    tri = jnp.concatenate([(row >= lane), jnp.ones((BLOCK, LANES), jnp.bool_)], axis=1)
    tri = jnp.where(tri, 1.0, 0.0).astype(BF16)
    tri_ref[...] = jnp.concatenate([tri, tri], axis=0)

    def blocks(js, diagonal_first):
        rows = [pl.ds(pl.multiple_of(j * BLOCK, BLOCK), BLOCK) for j in js]
        cols = [slice(p * LANES, (p + 1) * LANES) for p in pairs]
        steps = range(len(js))
        zs = [[_dot_nt(qq_ref[p], k_ref[0, rows[e], cols[p]]) for p in pairs] for e in steps]
        sums = [[None] * npair for _ in steps]
        for e in steps:
            for p in pairs:
                z = zs[e][p]
                soft = jnp.maximum(z, 0.0) + jnp.log(1.0 + jnp.exp2(-jnp.abs(z))) * LOG2E
                if diagonal_first and e == 0:
                    soft = jnp.where(causal, soft, 0.0)
                hi = soft.astype(BF16)
                lo = (soft - hi.astype(F32)).astype(BF16)
                sums[e][p] = _dot(jnp.concatenate([hi, lo], axis=1), tri_ref[...])
        least = None
        for p in pairs:
            carry = None if diagonal_first else carry_ref[p]
            pws, v2s = [], []
            for e in steps:
                after, total = sums[e][p][:, :LANES], sums[e][p][:, LANES:]
                if carry is None:
                    carry = total
                else:
                    after = after + carry
                    carry = carry + total
                weight = jnp.exp2(zs[e][p] - after)
                if diagonal_first and e == 0:
                    weight = jnp.where(causal, weight, 0.0)
                weight = weight.astype(BF16)
                pws.append(jnp.concatenate([weight[:BLOCK], weight[BLOCK:]], axis=1))
                v = v_ref[0, rows[e], cols[p]]
                v2s.append(jnp.concatenate([jnp.where(first_head, v, zero), jnp.where(first_head, zero, v)],
                                           axis=0))
            carry_ref[p] = carry
            least = carry if least is None else jnp.minimum(least, carry)
            out = _dot(jnp.concatenate(pws, axis=1), jnp.concatenate(v2s, axis=0))
            if diagonal_first:
                acc_ref[p] = out
            else:
                acc_ref[p] += out
        return jnp.min(least)

    def q_block(i, _):
        rows = pl.ds(pl.multiple_of(i * BLOCK, BLOCK), BLOCK)
        for p in pairs:
            q = q_ref[0, rows, p * LANES:(p + 1) * LANES]
            qq_ref[p] = jnp.concatenate([jnp.where(first_head, q, zero), jnp.where(first_head, zero, q)], axis=0)
        head = 3
        least_carry = lax.cond(i >= head - 1,
                               lambda: blocks([i - e for e in range(head)], True),
                               lambda: blocks([i], True))
        done = jnp.where(i >= head - 1, head - 1, 0)

        def more(state):
            t, least_carry = state
            return jnp.logical_and(t < i, least_carry < EXP2_UNDERFLOW)

        def kv_step(state):
            t, _ = state
            return t + 1, blocks([i - 1 - t], False)

        lax.while_loop(more, kv_step, (done.astype(jnp.int32), least_carry))
        for p in pairs:
            o_ref[0, rows, p * LANES:(p + 1) * LANES] = acc_ref[p].astype(o_ref.dtype)
        return 0

    lax.fori_loop(0, nblk, q_block, 0)


def _stick(sb):
    B, S, _ = sb.shape
    npair = SB_W // LANES
    spec = lambda c: pl.BlockSpec((1, S, SB_W), lambda b: (b, 0, c))
    return pl.pallas_call(
        _stick_kernel,
        grid=(B,),
        in_specs=[spec(0), spec(1), spec(2)],
        out_specs=spec(0),
        out_shape=jax.ShapeDtypeStruct((B, S, SB_W), BF16),
        scratch_shapes=[pltpu.VMEM((npair, 2 * BLOCK, LANES), BF16),
                        pltpu.VMEM((2 * BLOCK, 2 * LANES), BF16),
                        pltpu.VMEM((npair, 2 * BLOCK, LANES), F32),
                        pltpu.VMEM((npair, BLOCK, LANES), F32)],
        compiler_params=_params("parallel"),
        name="stick",
    )(sb, sb, sb)


def _dilated_kernel(d0_ref, d1_ref, d2_ref, o_ref, cap_ref, og_ref, lse_ref):
    S = o_ref.shape[1]
    nchunk = DIL_W // LANES
    span = BLOCK
    assert all(window // dil == span for window, dil in DIL_GROUPS)
    lane_qk = lax.broadcasted_iota(jnp.int32, (BLOCK, DIL_W), 1)
    qk_head = (lane_qk % LANES) // (HEAD_DIM // 2)
    v_head = lane_qk // HEAD_DIM
    zero = jnp.zeros((), BF16)

    qi = lax.broadcasted_iota(jnp.int32, (BLOCK, 2 * BLOCK), 0)
    kj = lax.broadcasted_iota(jnp.int32, (BLOCK, 2 * BLOCK), 1)
    for a in range(2):
        dist = qi + a * BLOCK - kj
        cap_ref[a] = jnp.where((dist >= 0) & (dist <= span), float(jnp.finfo(F32).max), NEG_INF)

    def key_window(ref, i):
        L = ref.shape[2]
        nk = min(2 * BLOCK, L)
        first = jnp.maximum(i - 1, 0) if nk < L else 0
        return first, nk

    def scores(ref, r, i):
        first, nk = key_window(ref, i)
        q = ref[0, r, pl.ds(i * BLOCK, BLOCK), 0:DIL_W]
        kk = ref[0, r, pl.ds(first * BLOCK, nk), DIL_W:2 * DIL_W]
        qs = jnp.concatenate([jnp.where(qk_head == h, q, zero) for h in range(DIL_HEADS)], axis=0)
        return _dot_nt(qs, kk)

    def attend(g, ref, dil, r, i, s):
        first, nk = key_window(ref, i)
        vv = ref[0, r, pl.ds(first * BLOCK, nk), 2 * DIL_W:3 * DIL_W]
        cap = cap_ref[i - first, :, 0:nk]
        s = jnp.minimum(s, jnp.concatenate([cap] * DIL_HEADS, axis=0))
        m = jnp.max(s, axis=-1, keepdims=True)
        p = jnp.exp2(s - m)
        den = jnp.sum(p, axis=-1, keepdims=True)
        pv = _dot(p.astype(BF16), vv) * (1.0 / den)
        lse_rows = m * LN2 + jnp.log(den)
        out = pv[:BLOCK]
        lse = jnp.broadcast_to(lse_rows[:BLOCK], (BLOCK, DIL_W))
        for h in range(1, DIL_HEADS):
            out = jnp.where(v_head == h, pv[h * BLOCK:(h + 1) * BLOCK], out)
            lse = jnp.where(v_head == h, lse_rows[h * BLOCK:(h + 1) * BLOCK], lse)
        rows = pl.ds(r + dil * BLOCK * i, BLOCK, stride=dil) if dil > 1 else pl.ds(i * BLOCK, BLOCK)
        for c in range(nchunk):
            og_ref[g, c, rows, :] = out[:, c * LANES:(c + 1) * LANES]
            lse_ref[g, c, rows, :] = lse[:, c * LANES:(c + 1) * LANES]

    side = 4
    for g, ((_, dil), ref) in enumerate(zip(DIL_GROUPS, (d0_ref, d1_ref, d2_ref))):
        nb = (S // dil) // BLOCK
        units = dil * nb
        assert units % side == 0

        def unit_group(t, _, g=g, ref=ref, dil=dil, nb=nb):
            where = [((side * t + u) // nb, (side * t + u) % nb) for u in range(side)]
            ss = [scores(ref, r, i) for r, i in where]
            for (r, i), s in zip(where, ss):
                attend(g, ref, dil, r, i, s)
            return 0

        lax.fori_loop(0, units // side, unit_group, 0)

    chunk = 2 * BLOCK

    def merge(t, _):
        rows = pl.ds(t * chunk, chunk)
        for c in range(nchunk):
            lses = [lse_ref[g, c, rows, :] for g in range(N_GROUPS)]
            top = functools.reduce(jnp.maximum, lses)
            ws = [jnp.exp(l - top) for l in lses]
            num = functools.reduce(lambda a, b: a + b, [w * og_ref[g, c, rows, :] for g, w in enumerate(ws)])
            den = functools.reduce(lambda a, b: a + b, ws)
            o_ref[0, rows, c * LANES:(c + 1) * LANES] = (num / den).astype(o_ref.dtype)
        return 0

    lax.fori_loop(0, S // chunk, merge, 0)


def _dilated(dils, S):
    B = dils[0].shape[0]
    return pl.pallas_call(
        _dilated_kernel,
        grid=(B,),
        in_specs=[pl.BlockSpec((1,) + d.shape[1:], lambda b: (b, 0, 0, 0)) for d in dils],
        out_specs=pl.BlockSpec((1, S, DIL_W), lambda b: (b, 0, 0)),
        out_shape=jax.ShapeDtypeStruct((B, S, DIL_W), BF16),
        scratch_shapes=[pltpu.VMEM((2, BLOCK, 2 * BLOCK), F32)]
                       + [pltpu.VMEM((N_GROUPS, DIL_W // LANES, S, LANES), F32)] * 2,
        compiler_params=_params("parallel"),
        name="dilated",
    )(*dils)


def _mix_kernel(x_ref, oa_ref, ob_ref, qc_ref, kv_ref, g_pre_ref, g_post_ref, w_gate_ref, b_gate_ref,
                w_a_ref, w_b_ref, w_c_ref, w_o_ref, out_ref):
    D = x_ref.shape[-1]
    for rows in _sub_tiles(x_ref.shape[1]):
        x = x_ref[0, rows, :]
        h = _rms_norm(x, g_pre_ref[...]).astype(BF16)

        heads = []
        for hd in range(MEM_HEADS):
            cols = slice(hd * MEM_HEAD_DIM, (hd + 1) * MEM_HEAD_DIM)
            s = _dot_nt(qc_ref[0, rows, cols], kv_ref[0, :, cols])
            p = jnp.exp(s - jnp.max(s, axis=-1, keepdims=True))
            den = jnp.sum(p, axis=-1, keepdims=True)
            v = kv_ref[0, :, MEM_W + hd * MEM_HEAD_DIM:MEM_W + (hd + 1) * MEM_HEAD_DIM]
            heads.append((_dot(p.astype(BF16), v) / den).astype(BF16))
        o_c = jnp.concatenate(heads, axis=-1)

        merged = None
        for br, (o, w_ref) in enumerate(((oa_ref[0, rows, :], w_a_ref), (ob_ref[0, rows, :], w_b_ref),
                                         (o_c, w_c_ref))):
            gcols = slice(br * D, (br + 1) * D)
            gate = jax.nn.sigmoid(_dot(h, w_gate_ref[:, gcols]) + b_gate_ref[:, gcols])
            term = gate * _dot(o, w_ref[...])
            merged = term if merged is None else merged + term
        mix = _dot(merged.astype(BF16), w_o_ref[...])
        out_ref[0, rows, :] = x + _rms_norm(mix, g_post_ref[...])


def _mix(x, o_a, o_b, q_c, kv_m, g_pre, g_post, w_gate, b_gate, w_a, w_b, w_c, w_o):
    B, S, D = x.shape
    tm = TOKEN_TILE
    tile = lambda w: pl.BlockSpec((1, tm, w), lambda b, t: (b, t, 0))
    consts = (g_pre, g_post, w_gate, b_gate, w_a, w_b, w_c, w_o)
    return pl.pallas_call(
        _mix_kernel,
        grid=(B, S // tm),
        in_specs=[tile(D), tile(SB_W), tile(DIL_W), tile(MEM_W),
                  pl.BlockSpec((1,) + kv_m.shape[1:], lambda b, t: (b, 0, 0))]
                 + [_const_spec(c.shape) for c in consts],
        out_specs=tile(D),
        out_shape=jax.ShapeDtypeStruct((B, S, D), F32),
        compiler_params=_params("parallel", "parallel"),
        name="mix",
    )(x, o_a, o_b, q_c, kv_m, *consts)


def _ffn_kernel(x_ref, g_pre_ref, g_post_ref, w_gate_ref, w_up_ref, w_out_ref, out_ref):
    x = x_ref[...]
    h = _rms_norm(x, g_pre_ref[...]).astype(BF16)
    f = jax.nn.silu(_dot(h, w_gate_ref[...])) * _dot(h, w_up_ref[...])
    f = _dot(f.astype(BF16), w_out_ref[...])
    out_ref[...] = x + _rms_norm(f, g_post_ref[...])


def _ffn(x, g_pre, g_post, w_gate, w_up, w_out):
    N, D = x.shape
    tm = FFN_TILE
    consts = (g_pre, g_post, w_gate, w_up, w_out)
    return pl.pallas_call(
        _ffn_kernel,
        grid=(N // tm,),
        in_specs=[pl.BlockSpec((tm, D), lambda t: (t, 0))] + [_const_spec(c.shape) for c in consts],
        out_specs=pl.BlockSpec((tm, D), lambda t: (t, 0)),
        out_shape=jax.ShapeDtypeStruct((N, D), F32),
        compiler_params=_params("parallel"),
        name="ffn",
    )(x, *consts)


def _rope_tables(S):
    half = HEAD_DIM // 2
    inv_freq = ROPE_THETA ** (-jnp.arange(half, dtype=F32) * 2.0 / HEAD_DIM)
    ang = jnp.arange(S, dtype=F32)[:, None] * inv_freq[None, :]
    reps = LANES // half
    return jnp.tile(jnp.cos(ang), (1, reps)), jnp.tile(jnp.sin(ang), (1, reps))


def _rotary_layout(w):
    D = w.shape[0]
    w = w.reshape(D, DIL_HEADS, 2, HEAD_DIM // 2)
    return w.transpose(0, 2, 1, 3).reshape(D, DIL_W)


def _split_w_in(w):
    scale = HEAD_DIM ** -0.5
    w_sb = jnp.concatenate([w[:, :SB_W] * scale, w[:, SB_W:3 * SB_W]], axis=1).astype(BF16)
    w_dil = []
    off = 3 * SB_W
    for _ in DIL_GROUPS:
        q, k, v = (w[:, off + i * DIL_W:off + (i + 1) * DIL_W] for i in range(3))
        w_dil.append(jnp.concatenate([_rotary_layout(q) * scale, _rotary_layout(k), v], axis=1).astype(BF16))
        off += 3 * DIL_W
    w_qc = w[:, off:off + MEM_W].astype(BF16)
    return w_sb, w_qc, w_dil


def kernel(x, mem, g_pre_mix, g_post_mix, g_pre_ffn, g_post_ffn, g_mem, w_in, w_mem_kv, w_br_sb, w_br_dil,
           w_br_mem, w_gate, b_gate, w_o, w_ffn_in, w_ffn_out):
    B, S, D = x.shape
    depth = w_in.shape[0]
    d_ff = w_ffn_out.shape[1]
    cos, sin = _rope_tables(S)
    row = lambda v: v.reshape(1, -1)
    for l in range(depth):
        w_sb, w_qc, w_dil = _split_w_in(w_in[l])
        kv_m = _memkv(mem, row(g_mem[l]), w_mem_kv[l].astype(BF16))
        sb, q_c, *dils = _inproj(x, row(g_pre_mix[l]), cos, sin, w_sb, w_qc, w_dil)
        o_a = _stick(sb)
        o_b = _dilated(dils, S)
        x = _mix(x, o_a, o_b, q_c, kv_m, row(g_pre_mix[l]), row(g_post_mix[l]), w_gate[l].astype(BF16),
                 row(b_gate[l]), w_br_sb[l].astype(BF16), w_br_dil[l].astype(BF16), w_br_mem[l].astype(BF16),
                 w_o[l].astype(BF16))
        w_ff = w_ffn_in[l].astype(BF16)
        x = _ffn(x.reshape(B * S, D), row(g_pre_ffn[l]), row(g_post_ffn[l]), w_ff[:, :d_ff], w_ff[:, d_ff:],
                 w_ffn_out[l].astype(BF16)).reshape(B, S, D)
    return x
```

```python
import functools

import jax
import jax.numpy as jnp
from jax import lax
from jax.experimental import pallas as pl
from jax.experimental.pallas import tpu as pltpu

HEAD_DIM = 64
SB_HEADS = 8
DIL_GROUPS = ((128, 1), (512, 4), (2048, 16))
DIL_HEADS = 4
MEM_HEADS = 4
MEM_HEAD_DIM = 128
N_BRANCHES = 3
BLOCK = 128
ROPE_THETA = 10000.0
NORM_EPS = 1e-6
NEG_INF = -1e30

SB_W = SB_HEADS * HEAD_DIM
DIL_W = DIL_HEADS * HEAD_DIM
MEM_W = MEM_HEADS * MEM_HEAD_DIM
N_GROUPS = len(DIL_GROUPS)

LOG2E = 1.4426950408889634
LN2 = 0.6931471805599453
EXP2_UNDERFLOW = 127.0
QUERY_BLOCKS = 2
HEAD_BLOCKS = 3

LANES = 128
TOKEN_TILE = 1024
SUB_TILE = 512
FFN_TILE = 512
MEMKV_TILE = 1024
VMEM_LIMIT_BYTES = 56 * 1024 * 1024

BF16 = jnp.bfloat16
F32 = jnp.float32


def _dot(a, b):
    return jnp.dot(a, b, preferred_element_type=F32)


def _dot_nt(a, b):
    return lax.dot_general(a, b, (((1,), (1,)), ((), ())), preferred_element_type=F32)


def _rms_norm(x, g):
    return x * lax.rsqrt(jnp.mean(x * x, axis=-1, keepdims=True) + NORM_EPS) * g


def _const_spec(shape):
    return pl.BlockSpec(shape, lambda *_: (0,) * len(shape), pipeline_mode=pl.Buffered(1))


def _params(*semantics):
    return pltpu.CompilerParams(dimension_semantics=semantics, vmem_limit_bytes=VMEM_LIMIT_BYTES)


def _sub_tiles(tm):
    return [slice(s, s + SUB_TILE) for s in range(0, tm, SUB_TILE)]


def _memkv_kernel(mem_ref, g_ref, w_ref, out_ref):
    h = _rms_norm(mem_ref[...], g_ref[...]).astype(BF16)
    out_ref[...] = _dot(h, w_ref[...]).astype(BF16)


def _memkv(mem, g_mem, w_kv):
    B, M, D = mem.shape
    rows = B * M
    tm = min(MEMKV_TILE, rows)
    return pl.pallas_call(
        _memkv_kernel,
        grid=(rows // tm,),
        in_specs=[pl.BlockSpec((tm, D), lambda t: (t, 0)),
                  _const_spec((1, D)),
                  _const_spec(w_kv.shape)],
        out_specs=pl.BlockSpec((tm, 2 * MEM_W), lambda t: (t, 0)),
        out_shape=jax.ShapeDtypeStruct((rows, 2 * MEM_W), BF16),
        compiler_params=_params("parallel"),
        name="memkv",
    )(mem.reshape(rows, D), g_mem, w_kv).reshape(B, M, 2 * MEM_W)


def _inproj_kernel(x_ref, g_ref, cos_ref, sin_ref, w_sb_ref, w_qc_ref, w_d0_ref, w_d1_ref, w_d2_ref,
                   sb_ref, qc_ref, d0_ref, d1_ref, d2_ref, split_ref):
    def rot(t, cos, sin):
        t1, t2 = t[:, :LANES], t[:, LANES:]
        return jnp.concatenate([t1 * cos - t2 * sin, t2 * cos + t1 * sin], axis=-1)

    for s, rows in enumerate(_sub_tiles(x_ref.shape[1])):
        h = _rms_norm(x_ref[0, rows, :], g_ref[...]).astype(BF16)
        sb = _dot(h, w_sb_ref[...])
        sb_ref[0, rows, :] = jnp.concatenate([sb[:, :SB_W] * LOG2E, sb[:, SB_W:]], axis=-1).astype(BF16)
        qc_ref[0, rows, :] = (_dot(h, w_qc_ref[...]) * (MEM_HEAD_DIM ** -0.5)).astype(BF16)
        cos = cos_ref[rows, :]
        sin = sin_ref[rows, :]
        for (_, dil), w_ref, out_ref in zip(DIL_GROUPS, (w_d0_ref, w_d1_ref, w_d2_ref),
                                            (d0_ref, d1_ref, d2_ref)):
            p = _dot(h, w_ref[...])
            p = jnp.concatenate([rot(p[:, :DIL_W], cos, sin) * LOG2E, rot(p[:, DIL_W:2 * DIL_W], cos, sin),
                                 p[:, 2 * DIL_W:]], axis=-1)
            if dil == 1:
                out_ref[0, 0, rows, :] = p.astype(BF16)
            else:
                nchunk = p.shape[1] // LANES
                per = SUB_TILE // dil
                for c in range(nchunk):
                    split_ref[s, c] = p[:, c * LANES:(c + 1) * LANES]
                for r in range(dil):
                    strided = pl.ds(r, per, stride=dil)
                    out_ref[0, r, s * per:(s + 1) * per, :] = jnp.concatenate(
                        [split_ref[s, c, strided, :] for c in range(nchunk)], axis=-1).astype(BF16)


def _inproj(x, g, cos, sin, w_sb, w_qc, w_dil):
    B, S, D = x.shape
    tm = TOKEN_TILE
    nt = S // tm
    tile = lambda w: pl.BlockSpec((1, tm, w), lambda b, t: (b, t, 0))
    dil_specs = [pl.BlockSpec((1, d, tm // d, 3 * DIL_W), lambda b, t: (b, 0, t, 0)) for _, d in DIL_GROUPS]
    dil_shapes = [jax.ShapeDtypeStruct((B, d, S // d, 3 * DIL_W), BF16) for _, d in DIL_GROUPS]
    return pl.pallas_call(
        _inproj_kernel,
        grid=(B, nt),
        in_specs=[tile(D), _const_spec((1, D)),
                  pl.BlockSpec((tm, LANES), lambda b, t: (t, 0)),
                  pl.BlockSpec((tm, LANES), lambda b, t: (t, 0)),
                  _const_spec(w_sb.shape), _const_spec(w_qc.shape)] + [_const_spec(w.shape) for w in w_dil],
        out_specs=[tile(3 * SB_W), tile(MEM_W)] + dil_specs,
        out_shape=[jax.ShapeDtypeStruct((B, S, 3 * SB_W), BF16),
                   jax.ShapeDtypeStruct((B, S, MEM_W), BF16)] + dil_shapes,
        scratch_shapes=[pltpu.VMEM((tm // SUB_TILE, 3 * DIL_W // LANES, SUB_TILE, LANES), F32)],
        compiler_params=_params("parallel", "parallel"),
        name="inproj",
    )(x, g, cos, sin, w_sb, w_qc, *w_dil)


def _stick_kernel(q_ref, k_ref, v_ref, o_ref, qq_ref, tri_ref, carry_ref, acc_ref):
    nblk = q_ref.shape[1] // BLOCK
    npair = q_ref.shape[2] // LANES
    pairs = range(npair)
    lane = lax.broadcasted_iota(jnp.int32, (BLOCK, LANES), 1)
    row = lax.broadcasted_iota(jnp.int32, (BLOCK, LANES), 0)
    first_head = lane < HEAD_DIM
    causal = jnp.concatenate([lane < row, lane < row], axis=0)
    zero = jnp.zeros((), BF16)
    tri = jnp.concatenate([(row >= lane), jnp.ones((BLOCK, LANES), jnp.bool_)], axis=1)
    tri = jnp.where(tri, 1.0, 0.0).astype(BF16)
    tri_ref[...] = jnp.concatenate([tri, tri], axis=0)

    def blocks(groups):
        cols = [slice(p * LANES, (p + 1) * LANES) for p in pairs]
        krows = lambda j: pl.ds(pl.multiple_of(j * BLOCK, BLOCK), BLOCK)
        zs = [[[_dot_nt(qq_ref[u, p], k_ref[0, krows(j), cols[p]]) for p in pairs] for j in js]
              for u, js, _ in groups]
        sums = [[[None] * npair for _ in js] for _, js, _ in groups]
        for g, (u, js, diagonal_first) in enumerate(groups):
            for e in range(len(js)):
                for p in pairs:
                    z = zs[g][e][p]
                    soft = jnp.maximum(z, 0.0) + jnp.log(1.0 + jnp.exp2(-jnp.abs(z))) * LOG2E
                    if diagonal_first and e == 0:
                        soft = jnp.where(causal, soft, 0.0)
                    hi = soft.astype(BF16)
                    lo = (soft - hi.astype(F32)).astype(BF16)
                    sums[g][e][p] = _dot(jnp.concatenate([hi, lo], axis=1), tri_ref[...])
        leasts = []
        for g, (u, js, diagonal_first) in enumerate(groups):
            least = None
            for p in pairs:
                carry = None if diagonal_first else carry_ref[u, p]
                pws, v2s = [], []
                for e, j in enumerate(js):
                    after, total = sums[g][e][p][:, :LANES], sums[g][e][p][:, LANES:]
                    if carry is None:
                        carry = total
                    else:
                        after = after + carry
                        carry = carry + total
                    weight = jnp.exp2(zs[g][e][p] - after)
                    if diagonal_first and e == 0:
                        weight = jnp.where(causal, weight, 0.0)
                    weight = weight.astype(BF16)
                    pws.append(jnp.concatenate([weight[:BLOCK], weight[BLOCK:]], axis=1))
                    v = v_ref[0, krows(j), cols[p]]
                    v2s.append(jnp.concatenate([jnp.where(first_head, v, zero), jnp.where(first_head, zero, v)],
                                               axis=0))
                carry_ref[u, p] = carry
                least = carry if least is None else jnp.minimum(least, carry)
                out = _dot(jnp.concatenate(pws, axis=1), jnp.concatenate(v2s, axis=0))
                if diagonal_first:
                    acc_ref[u, p] = out
                else:
                    acc_ref[u, p] += out
            leasts.append(jnp.min(least))
        return leasts

    def q_blocks(first):
        slots = range(QUERY_BLOCKS)
        qrows = [pl.ds(pl.multiple_of((first + u) * BLOCK, BLOCK), BLOCK) for u in slots]
        for u in slots:
            for p in pairs:
                q = q_ref[0, qrows[u], p * LANES:(p + 1) * LANES]
                qq_ref[u, p] = jnp.concatenate([jnp.where(first_head, q, zero), jnp.where(first_head, zero, q)],
                                               axis=0)
        depth = [HEAD_BLOCKS if not isinstance(first, int) else min(HEAD_BLOCKS, first + u + 1) for u in slots]
        leasts = blocks([(u, [first + u - e for e in range(depth[u])], True) for u in slots])
        for u in slots:
            i = first + u

            def more(state, i=i):
                t, least_carry = state
                return jnp.logical_and(t < i, least_carry < EXP2_UNDERFLOW)

            def kv_step(state, i=i, u=u):
                t, _ = state
                return t + 1, blocks([(u, [i - 1 - t], False)])[0]

            lax.while_loop(more, kv_step, (jnp.int32(depth[u] - 1), leasts[u]))
            for p in pairs:
                o_ref[0, qrows[u], p * LANES:(p + 1) * LANES] = acc_ref[u, p].astype(o_ref.dtype)

    assert nblk % QUERY_BLOCKS == 0 and HEAD_BLOCKS <= QUERY_BLOCKS + 1
    q_blocks(0)

    def rest(t, _):
        q_blocks(t * QUERY_BLOCKS)
        return 0

    lax.fori_loop(1, nblk // QUERY_BLOCKS, rest, 0)


def _stick(sb):
    B, S, _ = sb.shape
    npair = SB_W // LANES
    spec = lambda c: pl.BlockSpec((1, S, SB_W), lambda b: (b, 0, c))
    return pl.pallas_call(
        _stick_kernel,
        grid=(B,),
        in_specs=[spec(0), spec(1), spec(2)],
        out_specs=spec(0),
        out_shape=jax.ShapeDtypeStruct((B, S, SB_W), BF16),
        scratch_shapes=[pltpu.VMEM((QUERY_BLOCKS, npair, 2 * BLOCK, LANES), BF16),
                        pltpu.VMEM((2 * BLOCK, 2 * LANES), BF16),
                        pltpu.VMEM((QUERY_BLOCKS, npair, 2 * BLOCK, LANES), F32),
                        pltpu.VMEM((QUERY_BLOCKS, npair, BLOCK, LANES), F32)],
        compiler_params=_params("parallel"),
        name="stick",
    )(sb, sb, sb)


def _dilated_kernel(d0_ref, d1_ref, d2_ref, o_ref, cap_ref, og_ref, lse_ref):
    S = o_ref.shape[1]
    nchunk = DIL_W // LANES
    span = BLOCK
    assert all(window // dil == span for window, dil in DIL_GROUPS)
    lane_qk = lax.broadcasted_iota(jnp.int32, (BLOCK, DIL_W), 1)
    qk_head = (lane_qk % LANES) // (HEAD_DIM // 2)
    first_head = lax.broadcasted_iota(jnp.int32, (BLOCK, LANES), 1) < HEAD_DIM
    zero = jnp.zeros((), BF16)

    qi = lax.broadcasted_iota(jnp.int32, (BLOCK, 2 * BLOCK), 0)
    kj = lax.broadcasted_iota(jnp.int32, (BLOCK, 2 * BLOCK), 1)
    for a in range(2):
        dist = qi + a * BLOCK - kj
        cap_ref[a] = jnp.where((dist >= 0) & (dist <= span), float(jnp.finfo(F32).max), NEG_INF)

    def key_window(ref, i):
        L = ref.shape[2]
        nk = min(2 * BLOCK, L)
        first = jnp.maximum(i - 1, 0) if nk < L else 0
        return first, nk

    def scores(ref, r, i):
        first, nk = key_window(ref, i)
        q = ref[0, r, pl.ds(i * BLOCK, BLOCK), 0:DIL_W]
        kk = ref[0, r, pl.ds(first * BLOCK, nk), DIL_W:2 * DIL_W]
        qs = jnp.concatenate([jnp.where(qk_head == h, q, zero) for h in range(DIL_HEADS)], axis=0)
        return _dot_nt(qs, kk)

    def attend(g, ref, dil, r, i, s):
        first, nk = key_window(ref, i)
        vv = ref[0, r, pl.ds(first * BLOCK, nk), 2 * DIL_W:3 * DIL_W]
        cap = cap_ref[i - first, :, 0:nk]
        s = jnp.minimum(s, jnp.concatenate([cap] * DIL_HEADS, axis=0))
        m = jnp.max(s, axis=-1, keepdims=True)
        p = jnp.exp2(s - m)
        den = jnp.sum(p, axis=-1, keepdims=True)
        p = p.astype(BF16)
        rden = 1.0 / den
        lse_rows = m * LN2 + jnp.log(den)
        rows = pl.ds(r + dil * BLOCK * i, BLOCK, stride=dil) if dil > 1 else pl.ds(i * BLOCK, BLOCK)
        per = LANES // HEAD_DIM
        for c in range(nchunk):
            top = slice(per * c * BLOCK, (per * c + 1) * BLOCK)
            bot = slice((per * c + 1) * BLOCK, (per * c + 2) * BLOCK)
            pv = _dot(p[per * c * BLOCK:per * (c + 1) * BLOCK], vv[:, c * LANES:(c + 1) * LANES])
            pv = pv * rden[per * c * BLOCK:per * (c + 1) * BLOCK]
            og_ref[g, c, rows, :] = jnp.where(first_head, pv[:BLOCK], pv[BLOCK:])
            lse_ref[g, c, rows, :] = jnp.where(first_head, lse_rows[top], lse_rows[bot])

    side = 4
    for g, ((_, dil), ref) in enumerate(zip(DIL_GROUPS, (d0_ref, d1_ref, d2_ref))):
        nb = (S // dil) // BLOCK
        units = dil * nb
        assert units % side == 0

        def unit_group(t, _, g=g, ref=ref, dil=dil, nb=nb):
            where = [((side * t + u) // nb, (side * t + u) % nb) for u in range(side)]
            ss = [scores(ref, r, i) for r, i in where]
            for (r, i), s in zip(where, ss):
                attend(g, ref, dil, r, i, s)
            return 0

        lax.fori_loop(0, units // side, unit_group, 0)

    chunk = 2 * BLOCK

    def merge(t, _):
        rows = pl.ds(t * chunk, chunk)
        for c in range(nchunk):
            lses = [lse_ref[g, c, rows, :] for g in range(N_GROUPS)]
            top = functools.reduce(jnp.maximum, lses)
            ws = [jnp.exp(l - top) for l in lses]
            num = functools.reduce(lambda a, b: a + b, [w * og_ref[g, c, rows, :] for g, w in enumerate(ws)])
            den = functools.reduce(lambda a, b: a + b, ws)
            o_ref[0, rows, c * LANES:(c + 1) * LANES] = (num / den).astype(o_ref.dtype)
        return 0

    lax.fori_loop(0, S // chunk, merge, 0)


def _dilated(dils, S):
    B = dils[0].shape[0]
    return pl.pallas_call(
        _dilated_kernel,
        grid=(B,),
        in_specs=[pl.BlockSpec((1,) + d.shape[1:], lambda b: (b, 0, 0, 0)) for d in dils],
        out_specs=pl.BlockSpec((1, S, DIL_W), lambda b: (b, 0, 0)),
        out_shape=jax.ShapeDtypeStruct((B, S, DIL_W), BF16),
        scratch_shapes=[pltpu.VMEM((2, BLOCK, 2 * BLOCK), F32)]
                       + [pltpu.VMEM((N_GROUPS, DIL_W // LANES, S, LANES), F32)] * 2,
        compiler_params=_params("parallel"),
        name="dilated",
    )(*dils)


def _mix_kernel(x_ref, oa_ref, ob_ref, qc_ref, kv_ref, g_pre_ref, g_post_ref, w_gate_ref, b_gate_ref,
                w_a_ref, w_b_ref, w_c_ref, w_o_ref, out_ref):
    D = x_ref.shape[-1]
    for rows in _sub_tiles(x_ref.shape[1]):
        x = x_ref[0, rows, :]
        h = _rms_norm(x, g_pre_ref[...]).astype(BF16)

        heads = []
        for hd in range(MEM_HEADS):
            cols = slice(hd * MEM_HEAD_DIM, (hd + 1) * MEM_HEAD_DIM)
            s = _dot_nt(qc_ref[0, rows, cols], kv_ref[0, :, cols])
            p = jnp.exp(s - jnp.max(s, axis=-1, keepdims=True))
            den = jnp.sum(p, axis=-1, keepdims=True)
            v = kv_ref[0, :, MEM_W + hd * MEM_HEAD_DIM:MEM_W + (hd + 1) * MEM_HEAD_DIM]
            heads.append((_dot(p.astype(BF16), v) / den).astype(BF16))
        o_c = jnp.concatenate(heads, axis=-1)

        merged = None
        for br, (o, w_ref) in enumerate(((oa_ref[0, rows, :], w_a_ref), (ob_ref[0, rows, :], w_b_ref),
                                         (o_c, w_c_ref))):
            gcols = slice(br * D, (br + 1) * D)
            gate = jax.nn.sigmoid(_dot(h, w_gate_ref[:, gcols]) + b_gate_ref[:, gcols])
            term = gate * _dot(o, w_ref[...])
            merged = term if merged is None else merged + term
        mix = _dot(merged.astype(BF16), w_o_ref[...])
        out_ref[0, rows, :] = x + _rms_norm(mix, g_post_ref[...])


def _mix(x, o_a, o_b, q_c, kv_m, g_pre, g_post, w_gate, b_gate, w_a, w_b, w_c, w_o):
    B, S, D = x.shape
    tm = TOKEN_TILE
    tile = lambda w: pl.BlockSpec((1, tm, w), lambda b, t: (b, t, 0))
    consts = (g_pre, g_post, w_gate, b_gate, w_a, w_b, w_c, w_o)
    return pl.pallas_call(
        _mix_kernel,
        grid=(B, S // tm),
        in_specs=[tile(D), tile(SB_W), tile(DIL_W), tile(MEM_W),
                  pl.BlockSpec((1,) + kv_m.shape[1:], lambda b, t: (b, 0, 0))]
                 + [_const_spec(c.shape) for c in consts],
        out_specs=tile(D),
        out_shape=jax.ShapeDtypeStruct((B, S, D), F32),
        compiler_params=_params("parallel", "parallel"),
        name="mix",
    )(x, o_a, o_b, q_c, kv_m, *consts)


def _ffn_kernel(x_ref, g_pre_ref, g_post_ref, w_gate_ref, w_up_ref, w_out_ref, out_ref):
    x = x_ref[...]
    h = _rms_norm(x, g_pre_ref[...]).astype(BF16)
    f = jax.nn.silu(_dot(h, w_gate_ref[...])) * _dot(h, w_up_ref[...])
    f = _dot(f.astype(BF16), w_out_ref[...])
    out_ref[...] = x + _rms_norm(f, g_post_ref[...])


def _ffn(x, g_pre, g_post, w_ff, w_out):
    N, D = x.shape
    d_ff = w_out.shape[0]
    tm = FFN_TILE
    half = lambda c: pl.BlockSpec((D, d_ff), lambda t: (0, c), pipeline_mode=pl.Buffered(1))
    return pl.pallas_call(
        _ffn_kernel,
        grid=(N // tm,),
        in_specs=[pl.BlockSpec((tm, D), lambda t: (t, 0)), _const_spec(g_pre.shape), _const_spec(g_post.shape),
                  half(0), half(1), _const_spec(w_out.shape)],
        out_specs=pl.BlockSpec((tm, D), lambda t: (t, 0)),
        out_shape=jax.ShapeDtypeStruct((N, D), F32),
        compiler_params=_params("parallel"),
        name="ffn",
    )(x, g_pre, g_post, w_ff, w_ff, w_out)


def _rope_tables(S):
    half = HEAD_DIM // 2
    inv_freq = ROPE_THETA ** (-jnp.arange(half, dtype=F32) * 2.0 / HEAD_DIM)
    ang = jnp.arange(S, dtype=F32)[:, None] * inv_freq[None, :]
    reps = LANES // half
    return jnp.tile(jnp.cos(ang), (1, reps)), jnp.tile(jnp.sin(ang), (1, reps))


def _rotary_layout(w):
    D = w.shape[0]
    w = w.reshape(D, DIL_HEADS, 2, HEAD_DIM // 2)
    return w.transpose(0, 2, 1, 3).reshape(D, DIL_W)


def _split_w_in(w):
    scale = HEAD_DIM ** -0.5
    w_sb = jnp.concatenate([w[:, :SB_W] * scale, w[:, SB_W:3 * SB_W]], axis=1).astype(BF16)
    w_dil = []
    off = 3 * SB_W
    for _ in DIL_GROUPS:
        q, k, v = (w[:, off + i * DIL_W:off + (i + 1) * DIL_W] for i in range(3))
        w_dil.append(jnp.concatenate([_rotary_layout(q) * scale, _rotary_layout(k), v], axis=1).astype(BF16))
        off += 3 * DIL_W
    w_qc = w[:, off:off + MEM_W].astype(BF16)
    return w_sb, w_qc, w_dil


def kernel(x, mem, g_pre_mix, g_post_mix, g_pre_ffn, g_post_ffn, g_mem, w_in, w_mem_kv, w_br_sb, w_br_dil,
           w_br_mem, w_gate, b_gate, w_o, w_ffn_in, w_ffn_out):
    B, S, D = x.shape
    depth = w_in.shape[0]
    d_ff = w_ffn_out.shape[1]
    cos, sin = _rope_tables(S)
    row = lambda v: v.reshape(1, -1)
    for l in range(depth):
        w_sb, w_qc, w_dil = _split_w_in(w_in[l])
        kv_m = _memkv(mem, row(g_mem[l]), w_mem_kv[l].astype(BF16))
        sb, q_c, *dils = _inproj(x, row(g_pre_mix[l]), cos, sin, w_sb, w_qc, w_dil)
        o_a = _stick(sb)
        o_b = _dilated(dils, S)
        x = _mix(x, o_a, o_b, q_c, kv_m, row(g_pre_mix[l]), row(g_post_mix[l]), w_gate[l].astype(BF16),
                 row(b_gate[l]), w_br_sb[l].astype(BF16), w_br_dil[l].astype(BF16), w_br_mem[l].astype(BF16),
                 w_o[l].astype(BF16))
        x = _ffn(x.reshape(B * S, D), row(g_pre_ffn[l]), row(g_post_ffn[l]), w_ffn_in[l].astype(BF16),
                 w_ffn_out[l].astype(BF16)).reshape(B, S, D)
    return x
```

```python
import functools

import jax
import jax.numpy as jnp
from jax import lax
from jax.experimental import pallas as pl
from jax.experimental.pallas import tpu as pltpu

HEAD_DIM = 64
SB_HEADS = 8
DIL_GROUPS = ((128, 1), (512, 4), (2048, 16))
DIL_HEADS = 4
MEM_HEADS = 4
MEM_HEAD_DIM = 128
N_BRANCHES = 3
BLOCK = 128
ROPE_THETA = 10000.0
NORM_EPS = 1e-6
NEG_INF = -1e30

SB_W = SB_HEADS * HEAD_DIM
DIL_W = DIL_HEADS * HEAD_DIM
MEM_W = MEM_HEADS * MEM_HEAD_DIM
N_GROUPS = len(DIL_GROUPS)

LOG2E = 1.4426950408889634
EXP2_UNDERFLOW = 127.0
QUERY_BLOCKS = 2
HEAD_BLOCKS = 3

LANES = 128
TOKEN_TILE = 1024
SUB_TILE = 512
FFN_TILE = 1024
MEMKV_TILE = 1024
VMEM_LIMIT_BYTES = 56 * 1024 * 1024

BF16 = jnp.bfloat16
F32 = jnp.float32


def _dot(a, b):
    return jnp.dot(a, b, preferred_element_type=F32)


def _dot_nt(a, b):
    return lax.dot_general(a, b, (((1,), (1,)), ((), ())), preferred_element_type=F32)


def _rms_norm(x, g):
    return x * lax.rsqrt(jnp.mean(x * x, axis=-1, keepdims=True) + NORM_EPS) * g


def _const_spec(shape):
    return pl.BlockSpec(shape, lambda *_: (0,) * len(shape), pipeline_mode=pl.Buffered(1))


def _params(*semantics):
    return pltpu.CompilerParams(dimension_semantics=semantics, vmem_limit_bytes=VMEM_LIMIT_BYTES)


def _sub_tiles(tm):
    return [slice(s, s + SUB_TILE) for s in range(0, tm, SUB_TILE)]


def _memkv_kernel(mem_ref, g_ref, w_ref, out_ref):
    h = _rms_norm(mem_ref[...], g_ref[...]).astype(BF16)
    out_ref[...] = _dot(h, w_ref[...]).astype(BF16)


def _memkv(mem, g_mem, w_kv):
    B, M, D = mem.shape
    rows = B * M
    tm = min(MEMKV_TILE, rows)
    return pl.pallas_call(
        _memkv_kernel,
        grid=(rows // tm,),
        in_specs=[pl.BlockSpec((tm, D), lambda t: (t, 0)),
                  _const_spec((1, D)),
                  _const_spec(w_kv.shape)],
        out_specs=pl.BlockSpec((tm, 2 * MEM_W), lambda t: (t, 0)),
        out_shape=jax.ShapeDtypeStruct((rows, 2 * MEM_W), BF16),
        compiler_params=_params("parallel"),
        name="memkv",
    )(mem.reshape(rows, D), g_mem, w_kv).reshape(B, M, 2 * MEM_W)


def _inproj_kernel(x_ref, g_ref, cos_ref, sin_ref, w_sb_ref, w_qc_ref, w_d0_ref, w_d1_ref, w_d2_ref,
                   sb_ref, qc_ref, d0_ref, d1_ref, d2_ref, split_ref):
    def rot(t, cos, sin):
        t1, t2 = t[:, :LANES], t[:, LANES:]
        return jnp.concatenate([t1 * cos - t2 * sin, t2 * cos + t1 * sin], axis=-1)

    for s, rows in enumerate(_sub_tiles(x_ref.shape[1])):
        h = _rms_norm(x_ref[0, rows, :], g_ref[...]).astype(BF16)
        sb = _dot(h, w_sb_ref[...])
        sb_ref[0, rows, :] = jnp.concatenate([sb[:, :SB_W] * LOG2E, sb[:, SB_W:]], axis=-1).astype(BF16)
        qc_ref[0, rows, :] = (_dot(h, w_qc_ref[...]) * (MEM_HEAD_DIM ** -0.5)).astype(BF16)
        cos = cos_ref[rows, :]
        sin = sin_ref[rows, :]
        for (_, dil), w_ref, out_ref in zip(DIL_GROUPS, (w_d0_ref, w_d1_ref, w_d2_ref),
                                            (d0_ref, d1_ref, d2_ref)):
            p = _dot(h, w_ref[...])
            p = jnp.concatenate([rot(p[:, :DIL_W], cos, sin) * LOG2E, rot(p[:, DIL_W:2 * DIL_W], cos, sin),
                                 p[:, 2 * DIL_W:]], axis=-1)
            if dil == 1:
                out_ref[0, 0, rows, :] = p.astype(BF16)
            else:
                nchunk = p.shape[1] // LANES
                per = SUB_TILE // dil
                for c in range(nchunk):
                    split_ref[s, c] = p[:, c * LANES:(c + 1) * LANES]
                for r in range(dil):
                    strided = pl.ds(r, per, stride=dil)
                    out_ref[0, r, s * per:(s + 1) * per, :] = jnp.concatenate(
                        [split_ref[s, c, strided, :] for c in range(nchunk)], axis=-1).astype(BF16)


def _inproj(x, g, cos, sin, w_sb, w_qc, w_dil):
    B, S, D = x.shape
    tm = TOKEN_TILE
    nt = S // tm
    tile = lambda w: pl.BlockSpec((1, tm, w), lambda b, t: (b, t, 0))
    dil_specs = [pl.BlockSpec((1, d, tm // d, 3 * DIL_W), lambda b, t: (b, 0, t, 0)) for _, d in DIL_GROUPS]
    dil_shapes = [jax.ShapeDtypeStruct((B, d, S // d, 3 * DIL_W), BF16) for _, d in DIL_GROUPS]
    return pl.pallas_call(
        _inproj_kernel,
        grid=(B, nt),
        in_specs=[tile(D), _const_spec((1, D)),
                  pl.BlockSpec((tm, LANES), lambda b, t: (t, 0)),
                  pl.BlockSpec((tm, LANES), lambda b, t: (t, 0)),
                  _const_spec(w_sb.shape), _const_spec(w_qc.shape)] + [_const_spec(w.shape) for w in w_dil],
        out_specs=[tile(3 * SB_W), tile(MEM_W)] + dil_specs,
        out_shape=[jax.ShapeDtypeStruct((B, S, 3 * SB_W), BF16),
                   jax.ShapeDtypeStruct((B, S, MEM_W), BF16)] + dil_shapes,
        scratch_shapes=[pltpu.VMEM((tm // SUB_TILE, 3 * DIL_W // LANES, SUB_TILE, LANES), F32)],
        compiler_params=_params("parallel", "parallel"),
        name="inproj",
    )(x, g, cos, sin, w_sb, w_qc, *w_dil)


def _stick_kernel(q_ref, k_ref, v_ref, o_ref, qq_ref, tri_ref, carry_ref, acc_ref):
    nblk = q_ref.shape[1] // BLOCK
    npair = q_ref.shape[2] // LANES
    pairs = range(npair)
    lane = lax.broadcasted_iota(jnp.int32, (BLOCK, LANES), 1)
    row = lax.broadcasted_iota(jnp.int32, (BLOCK, LANES), 0)
    first_head = lane < HEAD_DIM
    causal = jnp.concatenate([lane < row, lane < row], axis=0)
    zero = jnp.zeros((), BF16)
    tri = jnp.concatenate([(row >= lane), jnp.ones((BLOCK, LANES), jnp.bool_)], axis=1)
    tri = jnp.where(tri, 1.0, 0.0).astype(BF16)
    tri_ref[...] = jnp.concatenate([tri, tri], axis=0)

    def blocks(groups):
        cols = [slice(p * LANES, (p + 1) * LANES) for p in pairs]
        krows = lambda j: pl.ds(pl.multiple_of(j * BLOCK, BLOCK), BLOCK)
        zs = [[[_dot_nt(qq_ref[u, p], k_ref[0, krows(j), cols[p]]) for p in pairs] for j in js]
              for u, js, _ in groups]
        sums = [[[None] * npair for _ in js] for _, js, _ in groups]
        for g, (u, js, diagonal_first) in enumerate(groups):
            for e in range(len(js)):
                for p in pairs:
                    z = zs[g][e][p]
                    soft = jnp.maximum(z, 0.0) + jnp.log(1.0 + jnp.exp2(-jnp.abs(z))) * LOG2E
                    if diagonal_first and e == 0:
                        soft = jnp.where(causal, soft, 0.0)
                    hi = soft.astype(BF16)
                    lo = (soft - hi.astype(F32)).astype(BF16)
                    sums[g][e][p] = _dot(jnp.concatenate([hi, lo], axis=1), tri_ref[...])
        leasts = []
        for g, (u, js, diagonal_first) in enumerate(groups):
            least = None
            for p in pairs:
                carry = None if diagonal_first else carry_ref[u, p]
                pws, v2s = [], []
                for e, j in enumerate(js):
                    after, total = sums[g][e][p][:, :LANES], sums[g][e][p][:, LANES:]
                    if carry is None:
                        carry = total
                    else:
                        after = after + carry
                        carry = carry + total
                    weight = jnp.exp2(zs[g][e][p] - after)
                    if diagonal_first and e == 0:
                        weight = jnp.where(causal, weight, 0.0)
                    weight = weight.astype(BF16)
                    pws.append(jnp.concatenate([weight[:BLOCK], weight[BLOCK:]], axis=1))
                    v = v_ref[0, krows(j), cols[p]]
                    v2s.append(jnp.concatenate([jnp.where(first_head, v, zero), jnp.where(first_head, zero, v)],
                                               axis=0))
                carry_ref[u, p] = carry
                least = carry if least is None else jnp.minimum(least, carry)
                out = _dot(jnp.concatenate(pws, axis=1), jnp.concatenate(v2s, axis=0))
                if diagonal_first:
                    acc_ref[u, p] = out
                else:
                    acc_ref[u, p] += out
            leasts.append(jnp.min(least))
        return leasts

    def q_blocks(first):
        slots = range(QUERY_BLOCKS)
        qrows = [pl.ds(pl.multiple_of((first + u) * BLOCK, BLOCK), BLOCK) for u in slots]
        for u in slots:
            for p in pairs:
                q = q_ref[0, qrows[u], p * LANES:(p + 1) * LANES]
                qq_ref[u, p] = jnp.concatenate([jnp.where(first_head, q, zero), jnp.where(first_head, zero, q)],
                                               axis=0)
        depth = [HEAD_BLOCKS if not isinstance(first, int) else min(HEAD_BLOCKS, first + u + 1) for u in slots]
        leasts = blocks([(u, [first + u - e for e in range(depth[u])], True) for u in slots])
        for u in slots:
            i = first + u

            def more(state, i=i):
                t, least_carry = state
                return jnp.logical_and(t < i, least_carry < EXP2_UNDERFLOW)

            def kv_step(state, i=i, u=u):
                t, _ = state
                return t + 1, blocks([(u, [i - 1 - t], False)])[0]

            lax.while_loop(more, kv_step, (jnp.int32(depth[u] - 1), leasts[u]))
            for p in pairs:
                o_ref[0, qrows[u], p * LANES:(p + 1) * LANES] = acc_ref[u, p].astype(o_ref.dtype)

    assert nblk % QUERY_BLOCKS == 0 and HEAD_BLOCKS <= QUERY_BLOCKS + 1
    q_blocks(0)

    def rest(t, _):
        q_blocks(t * QUERY_BLOCKS)
        return 0

    lax.fori_loop(1, nblk // QUERY_BLOCKS, rest, 0)


def _stick(sb):
    B, S, _ = sb.shape
    npair = SB_W // LANES
    spec = lambda c: pl.BlockSpec((1, S, SB_W), lambda b: (b, 0, c))
    return pl.pallas_call(
        _stick_kernel,
        grid=(B,),
        in_specs=[spec(0), spec(1), spec(2)],
        out_specs=spec(0),
        out_shape=jax.ShapeDtypeStruct((B, S, SB_W), BF16),
        scratch_shapes=[pltpu.VMEM((QUERY_BLOCKS, npair, 2 * BLOCK, LANES), BF16),
                        pltpu.VMEM((2 * BLOCK, 2 * LANES), BF16),
                        pltpu.VMEM((QUERY_BLOCKS, npair, 2 * BLOCK, LANES), F32),
                        pltpu.VMEM((QUERY_BLOCKS, npair, BLOCK, LANES), F32)],
        compiler_params=_params("parallel"),
        name="stick",
    )(sb, sb, sb)


def _dilated_kernel(d0_ref, d1_ref, d2_ref, o_ref, cap_ref, og_ref, lse_ref):
    S = o_ref.shape[1]
    nchunk = DIL_W // LANES
    span = BLOCK
    assert all(window // dil == span for window, dil in DIL_GROUPS)
    lane_qk = lax.broadcasted_iota(jnp.int32, (BLOCK, DIL_W), 1)
    qk_head = (lane_qk % LANES) // (HEAD_DIM // 2)
    first_head = lax.broadcasted_iota(jnp.int32, (BLOCK, LANES), 1) < HEAD_DIM
    zero = jnp.zeros((), BF16)

    qi = lax.broadcasted_iota(jnp.int32, (BLOCK, 2 * BLOCK), 0)
    kj = lax.broadcasted_iota(jnp.int32, (BLOCK, 2 * BLOCK), 1)
    for a in range(2):
        dist = qi + a * BLOCK - kj
        cap_ref[a] = jnp.where((dist >= 0) & (dist <= span), float(jnp.finfo(F32).max), NEG_INF)

    def key_window(ref, i):
        L = ref.shape[2]
        nk = min(2 * BLOCK, L)
        first = jnp.maximum(i - 1, 0) if nk < L else 0
        return first, nk

    def scores(ref, r, i):
        first, nk = key_window(ref, i)
        q = ref[0, r, pl.ds(i * BLOCK, BLOCK), 0:DIL_W]
        kk = ref[0, r, pl.ds(first * BLOCK, nk), DIL_W:2 * DIL_W]
        qs = jnp.concatenate([jnp.where(qk_head == h, q, zero) for h in range(DIL_HEADS)], axis=0)
        return _dot_nt(qs, kk)

    def attend(g, ref, dil, r, i, s):
        first, nk = key_window(ref, i)
        vv = ref[0, r, pl.ds(first * BLOCK, nk), 2 * DIL_W:3 * DIL_W]
        cap = cap_ref[i - first, :, 0:nk]
        s = jnp.minimum(s, jnp.concatenate([cap] * DIL_HEADS, axis=0))
        m = jnp.max(s, axis=-1, keepdims=True)
        p = jnp.exp2(s - m)
        den = jnp.sum(p, axis=-1, keepdims=True)
        p = p.astype(BF16)
        rows = pl.ds(r + dil * BLOCK * i, BLOCK, stride=dil) if dil > 1 else pl.ds(i * BLOCK, BLOCK)
        per = LANES // HEAD_DIM
        for c in range(nchunk):
            top = slice(per * c * BLOCK, (per * c + 1) * BLOCK)
            bot = slice((per * c + 1) * BLOCK, (per * c + 2) * BLOCK)
            pv = _dot(p[per * c * BLOCK:per * (c + 1) * BLOCK], vv[:, c * LANES:(c + 1) * LANES])
            den_c = jnp.where(first_head, den[top], den[bot])
            m_c = jnp.where(first_head, m[top], m[bot])
            og_ref[g, c, rows, :] = jnp.where(first_head, pv[:BLOCK], pv[BLOCK:]) * (1.0 / den_c)
            lse_ref[g, c, rows, :] = m_c + jnp.log(den_c) * LOG2E

    side = 4
    for g, ((_, dil), ref) in enumerate(zip(DIL_GROUPS, (d0_ref, d1_ref, d2_ref))):
        nb = (S // dil) // BLOCK
        units = dil * nb
        assert units % side == 0

        def unit_group(t, _, g=g, ref=ref, dil=dil, nb=nb):
            where = [((side * t + u) // nb, (side * t + u) % nb) for u in range(side)]
            ss = [scores(ref, r, i) for r, i in where]
            for (r, i), s in zip(where, ss):
                attend(g, ref, dil, r, i, s)
            return 0

        lax.fori_loop(0, units // side, unit_group, 0)

    chunk = 2 * BLOCK

    def merge(t, _):
        rows = pl.ds(t * chunk, chunk)
        for c in range(nchunk):
            lses = [lse_ref[g, c, rows, :] for g in range(N_GROUPS)]
            top = functools.reduce(jnp.maximum, lses)
            ws = [jnp.exp2(l - top) for l in lses]
            num = functools.reduce(lambda a, b: a + b, [w * og_ref[g, c, rows, :] for g, w in enumerate(ws)])
            den = functools.reduce(lambda a, b: a + b, ws)
            o_ref[0, rows, c * LANES:(c + 1) * LANES] = (num / den).astype(o_ref.dtype)
        return 0

    lax.fori_loop(0, S // chunk, merge, 0)


def _dilated(dils, S):
    B = dils[0].shape[0]
    return pl.pallas_call(
        _dilated_kernel,
        grid=(B,),
        in_specs=[pl.BlockSpec((1,) + d.shape[1:], lambda b: (b, 0, 0, 0)) for d in dils],
        out_specs=pl.BlockSpec((1, S, DIL_W), lambda b: (b, 0, 0)),
        out_shape=jax.ShapeDtypeStruct((B, S, DIL_W), BF16),
        scratch_shapes=[pltpu.VMEM((2, BLOCK, 2 * BLOCK), F32)]
                       + [pltpu.VMEM((N_GROUPS, DIL_W // LANES, S, LANES), F32)] * 2,
        compiler_params=_params("parallel"),
        name="dilated",
    )(*dils)


def _mix_kernel(x_ref, oa_ref, ob_ref, qc_ref, kv_ref, g_pre_ref, g_post_ref, w_gate_ref, b_gate_ref,
                w_a_ref, w_b_ref, w_c_ref, w_o_ref, out_ref):
    D = x_ref.shape[-1]
    for rows in _sub_tiles(x_ref.shape[1]):
        x = x_ref[0, rows, :]
        h = _rms_norm(x, g_pre_ref[...]).astype(BF16)

        heads = []
        for hd in range(MEM_HEADS):
            cols = slice(hd * MEM_HEAD_DIM, (hd + 1) * MEM_HEAD_DIM)
            s = _dot_nt(qc_ref[0, rows, cols], kv_ref[0, :, cols])
            p = jnp.exp(s - jnp.max(s, axis=-1, keepdims=True))
            den = jnp.sum(p, axis=-1, keepdims=True)
            v = kv_ref[0, :, MEM_W + hd * MEM_HEAD_DIM:MEM_W + (hd + 1) * MEM_HEAD_DIM]
            heads.append((_dot(p.astype(BF16), v) / den).astype(BF16))
        o_c = jnp.concatenate(heads, axis=-1)

        merged = None
        for br, (o, w_ref) in enumerate(((oa_ref[0, rows, :], w_a_ref), (ob_ref[0, rows, :], w_b_ref),
                                         (o_c, w_c_ref))):
            gcols = slice(br * D, (br + 1) * D)
            gate = jax.nn.sigmoid(_dot(h, w_gate_ref[:, gcols]) + b_gate_ref[:, gcols])
            term = gate * _dot(o, w_ref[...])
            merged = term if merged is None else merged + term
        mix = _dot(merged.astype(BF16), w_o_ref[...])
        out_ref[0, rows, :] = x + _rms_norm(mix, g_post_ref[...])


def _mix(x, o_a, o_b, q_c, kv_m, g_pre, g_post, w_gate, b_gate, w_a, w_b, w_c, w_o):
    B, S, D = x.shape
    tm = TOKEN_TILE
    tile = lambda w: pl.BlockSpec((1, tm, w), lambda b, t: (b, t, 0))
    consts = (g_pre, g_post, w_gate, b_gate, w_a, w_b, w_c, w_o)
    return pl.pallas_call(
        _mix_kernel,
        grid=(B, S // tm),
        in_specs=[tile(D), tile(SB_W), tile(DIL_W), tile(MEM_W),
                  pl.BlockSpec((1,) + kv_m.shape[1:], lambda b, t: (b, 0, 0))]
                 + [_const_spec(c.shape) for c in consts],
        out_specs=tile(D),
        out_shape=jax.ShapeDtypeStruct((B, S, D), F32),
        compiler_params=_params("parallel", "parallel"),
        name="mix",
    )(x, o_a, o_b, q_c, kv_m, *consts)


def _ffn_kernel(x_ref, g_pre_ref, g_post_ref, w_gate_ref, w_up_ref, w_out_ref, out_ref):
    for rows in _sub_tiles(x_ref.shape[0]):
        x = x_ref[rows, :]
        h = _rms_norm(x, g_pre_ref[...]).astype(BF16)
        f = jax.nn.silu(_dot(h, w_gate_ref[...])) * _dot(h, w_up_ref[...])
        f = _dot(f.astype(BF16), w_out_ref[...])
        out_ref[rows, :] = x + _rms_norm(f, g_post_ref[...])


def _ffn(x, g_pre, g_post, w_ff, w_out):
    N, D = x.shape
    d_ff = w_out.shape[0]
    tm = FFN_TILE
    half = lambda c: pl.BlockSpec((D, d_ff), lambda t: (0, c), pipeline_mode=pl.Buffered(1))
    return pl.pallas_call(
        _ffn_kernel,
        grid=(N // tm,),
        in_specs=[pl.BlockSpec((tm, D), lambda t: (t, 0)), _const_spec(g_pre.shape), _const_spec(g_post.shape),
                  half(0), half(1), _const_spec(w_out.shape)],
        out_specs=pl.BlockSpec((tm, D), lambda t: (t, 0)),
        out_shape=jax.ShapeDtypeStruct((N, D), F32),
        compiler_params=_params("parallel"),
        name="ffn",
    )(x, g_pre, g_post, w_ff, w_ff, w_out)


def _rope_tables(S):
    half = HEAD_DIM // 2
    inv_freq = ROPE_THETA ** (-jnp.arange(half, dtype=F32) * 2.0 / HEAD_DIM)
    ang = jnp.arange(S, dtype=F32)[:, None] * inv_freq[None, :]
    reps = LANES // half
    return jnp.tile(jnp.cos(ang), (1, reps)), jnp.tile(jnp.sin(ang), (1, reps))


def _rotary_layout(w):
    D = w.shape[0]
    w = w.reshape(D, DIL_HEADS, 2, HEAD_DIM // 2)
    return w.transpose(0, 2, 1, 3).reshape(D, DIL_W)


def _split_w_in(w):
    scale = HEAD_DIM ** -0.5
    w_sb = jnp.concatenate([w[:, :SB_W] * scale, w[:, SB_W:3 * SB_W]], axis=1).astype(BF16)
    w_dil = []
    off = 3 * SB_W
    for _ in DIL_GROUPS:
        q, k, v = (w[:, off + i * DIL_W:off + (i + 1) * DIL_W] for i in range(3))
        w_dil.append(jnp.concatenate([_rotary_layout(q) * scale, _rotary_layout(k), v], axis=1).astype(BF16))
        off += 3 * DIL_W
    w_qc = w[:, off:off + MEM_W].astype(BF16)
    return w_sb, w_qc, w_dil


def kernel(x, mem, g_pre_mix, g_post_mix, g_pre_ffn, g_post_ffn, g_mem, w_in, w_mem_kv, w_br_sb, w_br_dil,
           w_br_mem, w_gate, b_gate, w_o, w_ffn_in, w_ffn_out):
    B, S, D = x.shape
    depth = w_in.shape[0]
    d_ff = w_ffn_out.shape[1]
    cos, sin = _rope_tables(S)
    row = lambda v: v.reshape(1, -1)
    for l in range(depth):
        w_sb, w_qc, w_dil = _split_w_in(w_in[l])
        kv_m = _memkv(mem, row(g_mem[l]), w_mem_kv[l].astype(BF16))
        sb, q_c, *dils = _inproj(x, row(g_pre_mix[l]), cos, sin, w_sb, w_qc, w_dil)
        o_a = _stick(sb)
        o_b = _dilated(dils, S)
        x = _mix(x, o_a, o_b, q_c, kv_m, row(g_pre_mix[l]), row(g_post_mix[l]), w_gate[l].astype(BF16),
                 row(b_gate[l]), w_br_sb[l].astype(BF16), w_br_dil[l].astype(BF16), w_br_mem[l].astype(BF16),
                 w_o[l].astype(BF16))
        x = _ffn(x.reshape(B * S, D), row(g_pre_ffn[l]), row(g_post_ffn[l]), w_ffn_in[l].astype(BF16),
                 w_ffn_out[l].astype(BF16)).reshape(B, S, D)
    return x
```

```python
import functools

import jax
import jax.numpy as jnp
from jax import lax
from jax.experimental import pallas as pl
from jax.experimental.pallas import tpu as pltpu

HEAD_DIM = 64
SB_HEADS = 8
DIL_GROUPS = ((128, 1), (512, 4), (2048, 16))
DIL_HEADS = 4
MEM_HEADS = 4
MEM_HEAD_DIM = 128
N_BRANCHES = 3
BLOCK = 128
ROPE_THETA = 10000.0
NORM_EPS = 1e-6
NEG_INF = -1e30

SB_W = SB_HEADS * HEAD_DIM
DIL_W = DIL_HEADS * HEAD_DIM
MEM_W = MEM_HEADS * MEM_HEAD_DIM
N_GROUPS = len(DIL_GROUPS)

LOG2E = 1.4426950408889634
EXP2_UNDERFLOW = 127.0
QUERY_BLOCKS = 2
HEAD_BLOCKS = 2

LANES = 128
TOKEN_TILE = 1024
SUB_TILE = 512
FFN_TILE = 1024
MEMKV_TILE = 1024
VMEM_LIMIT_BYTES = 56 * 1024 * 1024

BF16 = jnp.bfloat16
F32 = jnp.float32


def _dot(a, b):
    return jnp.dot(a, b, preferred_element_type=F32)


def _dot_nt(a, b):
    return lax.dot_general(a, b, (((1,), (1,)), ((), ())), preferred_element_type=F32)


def _rms_norm(x, g):
    return x * lax.rsqrt(jnp.mean(x * x, axis=-1, keepdims=True) + NORM_EPS) * g


def _const_spec(shape):
    return pl.BlockSpec(shape, lambda *_: (0,) * len(shape), pipeline_mode=pl.Buffered(1))


def _params(*semantics):
    return pltpu.CompilerParams(dimension_semantics=semantics, vmem_limit_bytes=VMEM_LIMIT_BYTES)


def _sub_tiles(tm):
    return [slice(s, s + SUB_TILE) for s in range(0, tm, SUB_TILE)]


def _memkv_kernel(mem_ref, g_ref, w_ref, out_ref):
    h = _rms_norm(mem_ref[...], g_ref[...]).astype(BF16)
    out_ref[...] = _dot(h, w_ref[...]).astype(BF16)


def _memkv(mem, g_mem, w_kv):
    B, M, D = mem.shape
    rows = B * M
    tm = min(MEMKV_TILE, rows)
    return pl.pallas_call(
        _memkv_kernel,
        grid=(rows // tm,),
        in_specs=[pl.BlockSpec((tm, D), lambda t: (t, 0)),
                  _const_spec((1, D)),
                  _const_spec(w_kv.shape)],
        out_specs=pl.BlockSpec((tm, 2 * MEM_W), lambda t: (t, 0)),
        out_shape=jax.ShapeDtypeStruct((rows, 2 * MEM_W), BF16),
        compiler_params=_params("parallel"),
        name="memkv",
    )(mem.reshape(rows, D), g_mem, w_kv).reshape(B, M, 2 * MEM_W)


def _inproj_kernel(x_ref, g_ref, cos_ref, sin_ref, w_sb_ref, w_qc_ref, w_d0_ref, w_d1_ref, w_d2_ref,
                   sb_ref, qc_ref, d0_ref, d1_ref, d2_ref, split_ref):
    def rot(t, cos, sin):
        t1, t2 = t[:, :LANES], t[:, LANES:]
        return jnp.concatenate([t1 * cos - t2 * sin, t2 * cos + t1 * sin], axis=-1)

    for s, rows in enumerate(_sub_tiles(x_ref.shape[1])):
        h = _rms_norm(x_ref[0, rows, :], g_ref[...]).astype(BF16)
        sb = _dot(h, w_sb_ref[...])
        sb_ref[0, rows, :] = jnp.concatenate([sb[:, :SB_W] * LOG2E, sb[:, SB_W:]], axis=-1).astype(BF16)
        qc_ref[0, rows, :] = (_dot(h, w_qc_ref[...]) * (MEM_HEAD_DIM ** -0.5)).astype(BF16)
        cos = cos_ref[rows, :]
        sin = sin_ref[rows, :]
        for (_, dil), w_ref, out_ref in zip(DIL_GROUPS, (w_d0_ref, w_d1_ref, w_d2_ref),
                                            (d0_ref, d1_ref, d2_ref)):
            p = _dot(h, w_ref[...])
            p = jnp.concatenate([rot(p[:, :DIL_W], cos, sin) * LOG2E, rot(p[:, DIL_W:2 * DIL_W], cos, sin),
                                 p[:, 2 * DIL_W:]], axis=-1)
            if dil == 1:
                out_ref[0, 0, rows, :] = p.astype(BF16)
            else:
                nchunk = p.shape[1] // LANES
                per = SUB_TILE // dil
                for c in range(nchunk):
                    split_ref[s, c] = p[:, c * LANES:(c + 1) * LANES]
                for r in range(dil):
                    strided = pl.ds(r, per, stride=dil)
                    out_ref[0, r, s * per:(s + 1) * per, :] = jnp.concatenate(
                        [split_ref[s, c, strided, :] for c in range(nchunk)], axis=-1).astype(BF16)


def _inproj(x, g, cos, sin, w_sb, w_qc, w_dil):
    B, S, D = x.shape
    tm = TOKEN_TILE
    nt = S // tm
    tile = lambda w: pl.BlockSpec((1, tm, w), lambda b, t: (b, t, 0))
    dil_specs = [pl.BlockSpec((1, d, tm // d, 3 * DIL_W), lambda b, t: (b, 0, t, 0)) for _, d in DIL_GROUPS]
    dil_shapes = [jax.ShapeDtypeStruct((B, d, S // d, 3 * DIL_W), BF16) for _, d in DIL_GROUPS]
    return pl.pallas_call(
        _inproj_kernel,
        grid=(B, nt),
        in_specs=[tile(D), _const_spec((1, D)),
                  pl.BlockSpec((tm, LANES), lambda b, t: (t, 0)),
                  pl.BlockSpec((tm, LANES), lambda b, t: (t, 0)),
                  _const_spec(w_sb.shape), _const_spec(w_qc.shape)] + [_const_spec(w.shape) for w in w_dil],
        out_specs=[tile(3 * SB_W), tile(MEM_W)] + dil_specs,
        out_shape=[jax.ShapeDtypeStruct((B, S, 3 * SB_W), BF16),
                   jax.ShapeDtypeStruct((B, S, MEM_W), BF16)] + dil_shapes,
        scratch_shapes=[pltpu.VMEM((tm // SUB_TILE, 3 * DIL_W // LANES, SUB_TILE, LANES), F32)],
        compiler_params=_params("parallel", "parallel"),
        name="inproj",
    )(x, g, cos, sin, w_sb, w_qc, *w_dil)


def _stick_kernel(q_ref, k_ref, v_ref, o_ref, qq_ref, tri_ref, carry_ref, acc_ref):
    nblk = q_ref.shape[1] // BLOCK
    npair = q_ref.shape[2] // LANES
    pairs = range(npair)
    lane = lax.broadcasted_iota(jnp.int32, (BLOCK, LANES), 1)
    row = lax.broadcasted_iota(jnp.int32, (BLOCK, LANES), 0)
    first_head = lane < HEAD_DIM
    causal = jnp.concatenate([lane < row, lane < row], axis=0)
    zero = jnp.zeros((), BF16)
    tri = jnp.concatenate([(row >= lane), jnp.ones((BLOCK, LANES), jnp.bool_)], axis=1)
    tri = jnp.where(tri, 1.0, 0.0).astype(BF16)
    tri_ref[...] = jnp.concatenate([tri, tri], axis=0)

    def blocks(groups):
        cols = [slice(p * LANES, (p + 1) * LANES) for p in pairs]
        krows = lambda j: pl.ds(pl.multiple_of(j * BLOCK, BLOCK), BLOCK)
        zs = [[[_dot_nt(qq_ref[u, p], k_ref[0, krows(j), cols[p]]) for p in pairs] for j in js]
              for u, js, _ in groups]
        sums = [[[None] * npair for _ in js] for _, js, _ in groups]
        for g, (u, js, diagonal_first) in enumerate(groups):
            for e in range(len(js)):
                for p in pairs:
                    z = zs[g][e][p]
                    soft = jnp.maximum(z, 0.0) + jnp.log(1.0 + jnp.exp2(-jnp.abs(z))) * LOG2E
                    if diagonal_first and e == 0:
                        soft = jnp.where(causal, soft, 0.0)
                    hi = soft.astype(BF16)
                    lo = (soft - hi.astype(F32)).astype(BF16)
                    sums[g][e][p] = _dot(jnp.concatenate([hi, lo], axis=1), tri_ref[...])
        leasts = []
        for g, (u, js, diagonal_first) in enumerate(groups):
            least = None
            for p in pairs:
                carry = None if diagonal_first else carry_ref[u, p]
                pws, v2s = [], []
                for e, j in enumerate(js):
                    after, total = sums[g][e][p][:, :LANES], sums[g][e][p][:, LANES:]
                    if carry is None:
                        carry = total
                    else:
                        after = after + carry
                        carry = carry + total
                    weight = jnp.exp2(zs[g][e][p] - after)
                    if diagonal_first and e == 0:
                        weight = jnp.where(causal, weight, 0.0)
                    weight = weight.astype(BF16)
                    pws.append(jnp.concatenate([weight[:BLOCK], weight[BLOCK:]], axis=1))
                    v = v_ref[0, krows(j), cols[p]]
                    v2s.append(jnp.concatenate([jnp.where(first_head, v, zero), jnp.where(first_head, zero, v)],
                                               axis=0))
                carry_ref[u, p] = carry
                least = carry if least is None else jnp.minimum(least, carry)
                out = _dot(jnp.concatenate(pws, axis=1), jnp.concatenate(v2s, axis=0))
                if diagonal_first:
                    acc_ref[u, p] = out
                else:
                    acc_ref[u, p] += out
            leasts.append(jnp.min(least))
        return leasts

    def q_blocks(first):
        slots = range(QUERY_BLOCKS)
        qrows = [pl.ds(pl.multiple_of((first + u) * BLOCK, BLOCK), BLOCK) for u in slots]
        for u in slots:
            for p in pairs:
                q = q_ref[0, qrows[u], p * LANES:(p + 1) * LANES]
                qq_ref[u, p] = jnp.concatenate([jnp.where(first_head, q, zero), jnp.where(first_head, zero, q)],
                                               axis=0)
        depth = [HEAD_BLOCKS if not isinstance(first, int) else min(HEAD_BLOCKS, first + u + 1) for u in slots]
        leasts = blocks([(u, [first + u - e for e in range(depth[u])], True) for u in slots])
        for u in slots:
            i = first + u

            def more(state, i=i):
                t, least_carry = state
                return jnp.logical_and(t < i, least_carry < EXP2_UNDERFLOW)

            def kv_step(state, i=i, u=u):
                t, _ = state
                return t + 1, blocks([(u, [i - 1 - t], False)])[0]

            lax.while_loop(more, kv_step, (jnp.int32(depth[u] - 1), leasts[u]))
            for p in pairs:
                o_ref[0, qrows[u], p * LANES:(p + 1) * LANES] = acc_ref[u, p].astype(o_ref.dtype)

    assert nblk % QUERY_BLOCKS == 0 and HEAD_BLOCKS <= QUERY_BLOCKS + 1
    q_blocks(0)

    def rest(t, _):
        q_blocks(t * QUERY_BLOCKS)
        return 0

    lax.fori_loop(1, nblk // QUERY_BLOCKS, rest, 0)


def _stick(sb):
    B, S, _ = sb.shape
    npair = SB_W // LANES
    spec = lambda c: pl.BlockSpec((1, S, SB_W), lambda b: (b, 0, c))
    return pl.pallas_call(
        _stick_kernel,
        grid=(B,),
        in_specs=[spec(0), spec(1), spec(2)],
        out_specs=spec(0),
        out_shape=jax.ShapeDtypeStruct((B, S, SB_W), BF16),
        scratch_shapes=[pltpu.VMEM((QUERY_BLOCKS, npair, 2 * BLOCK, LANES), BF16),
                        pltpu.VMEM((2 * BLOCK, 2 * LANES), BF16),
                        pltpu.VMEM((QUERY_BLOCKS, npair, 2 * BLOCK, LANES), F32),
                        pltpu.VMEM((QUERY_BLOCKS, npair, BLOCK, LANES), F32)],
        compiler_params=_params("parallel"),
        name="stick",
    )(sb, sb, sb)


def _dilated_kernel(d0_ref, d1_ref, d2_ref, o_ref, cap_ref, og_ref, lse_ref):
    S = o_ref.shape[1]
    nchunk = DIL_W // LANES
    span = BLOCK
    assert all(window // dil == span for window, dil in DIL_GROUPS)
    lane_qk = lax.broadcasted_iota(jnp.int32, (BLOCK, DIL_W), 1)
    qk_head = (lane_qk % LANES) // (HEAD_DIM // 2)
    first_head = lax.broadcasted_iota(jnp.int32, (BLOCK, LANES), 1) < HEAD_DIM
    zero = jnp.zeros((), BF16)

    qi = lax.broadcasted_iota(jnp.int32, (BLOCK, 2 * BLOCK), 0)
    kj = lax.broadcasted_iota(jnp.int32, (BLOCK, 2 * BLOCK), 1)
    for a in range(2):
        dist = qi + a * BLOCK - kj
        cap_ref[a] = jnp.where((dist >= 0) & (dist <= span), float(jnp.finfo(F32).max), NEG_INF)

    def key_window(ref, i):
        L = ref.shape[2]
        nk = min(2 * BLOCK, L)
        first = jnp.maximum(i - 1, 0) if nk < L else 0
        return first, nk

    def scores(ref, r, i):
        first, nk = key_window(ref, i)
        q = ref[0, r, pl.ds(i * BLOCK, BLOCK), 0:DIL_W]
        kk = ref[0, r, pl.ds(first * BLOCK, nk), DIL_W:2 * DIL_W]
        qs = jnp.concatenate([jnp.where(qk_head == h, q, zero) for h in range(DIL_HEADS)], axis=0)
        return _dot_nt(qs, kk)

    def attend(g, ref, dil, r, i, s):
        first, nk = key_window(ref, i)
        vv = ref[0, r, pl.ds(first * BLOCK, nk), 2 * DIL_W:3 * DIL_W]
        cap = cap_ref[i - first, :, 0:nk]
        s = jnp.minimum(s, jnp.concatenate([cap] * DIL_HEADS, axis=0))
        m = jnp.max(s, axis=-1, keepdims=True)
        p = jnp.exp2(s - m)
        den = jnp.sum(p, axis=-1, keepdims=True)
        p = p.astype(BF16)
        rows = pl.ds(r + dil * BLOCK * i, BLOCK, stride=dil) if dil > 1 else pl.ds(i * BLOCK, BLOCK)
        per = LANES // HEAD_DIM
        for c in range(nchunk):
            top = slice(per * c * BLOCK, (per * c + 1) * BLOCK)
            bot = slice((per * c + 1) * BLOCK, (per * c + 2) * BLOCK)
            pv = _dot(p[per * c * BLOCK:per * (c + 1) * BLOCK], vv[:, c * LANES:(c + 1) * LANES])
            den_c = jnp.where(first_head, den[top], den[bot])
            m_c = jnp.where(first_head, m[top], m[bot])
            og_ref[g, c, rows, :] = jnp.where(first_head, pv[:BLOCK], pv[BLOCK:]) * (1.0 / den_c)
            lse_ref[g, c, rows, :] = m_c + jnp.log(den_c) * LOG2E

    side = 4
    for g, ((_, dil), ref) in enumerate(zip(DIL_GROUPS, (d0_ref, d1_ref, d2_ref))):
        nb = (S // dil) // BLOCK
        units = dil * nb
        assert units % side == 0

        def unit_group(t, _, g=g, ref=ref, dil=dil, nb=nb):
            where = [((side * t + u) // nb, (side * t + u) % nb) for u in range(side)]
            ss = [scores(ref, r, i) for r, i in where]
            for (r, i), s in zip(where, ss):
                attend(g, ref, dil, r, i, s)
            return 0

        lax.fori_loop(0, units // side, unit_group, 0)

    chunk = 2 * BLOCK

    def merge(t, _):
        rows = pl.ds(t * chunk, chunk)
        for c in range(nchunk):
            lses = [lse_ref[g, c, rows, :] for g in range(N_GROUPS)]
            top = functools.reduce(jnp.maximum, lses)
            ws = [jnp.exp2(l - top) for l in lses]
            num = functools.reduce(lambda a, b: a + b, [w * og_ref[g, c, rows, :] for g, w in enumerate(ws)])
            den = functools.reduce(lambda a, b: a + b, ws)
            o_ref[0, rows, c * LANES:(c + 1) * LANES] = (num / den).astype(o_ref.dtype)
        return 0

    lax.fori_loop(0, S // chunk, merge, 0)


def _dilated(dils, S):
    B = dils[0].shape[0]
    return pl.pallas_call(
        _dilated_kernel,
        grid=(B,),
        in_specs=[pl.BlockSpec((1,) + d.shape[1:], lambda b: (b, 0, 0, 0)) for d in dils],
        out_specs=pl.BlockSpec((1, S, DIL_W), lambda b: (b, 0, 0)),
        out_shape=jax.ShapeDtypeStruct((B, S, DIL_W), BF16),
        scratch_shapes=[pltpu.VMEM((2, BLOCK, 2 * BLOCK), F32)]
                       + [pltpu.VMEM((N_GROUPS, DIL_W // LANES, S, LANES), F32)] * 2,
        compiler_params=_params("parallel"),
        name="dilated",
    )(*dils)


def _mix_kernel(x_ref, oa_ref, ob_ref, qc_ref, kv_ref, g_pre_ref, g_post_ref, w_gate_ref, b_gate_ref,
                w_a_ref, w_b_ref, w_c_ref, w_o_ref, out_ref):
    D = x_ref.shape[-1]
    for rows in _sub_tiles(x_ref.shape[1]):
        x = x_ref[0, rows, :]
        h = _rms_norm(x, g_pre_ref[...]).astype(BF16)

        heads = []
        for hd in range(MEM_HEADS):
            cols = slice(hd * MEM_HEAD_DIM, (hd + 1) * MEM_HEAD_DIM)
            s = _dot_nt(qc_ref[0, rows, cols], kv_ref[0, :, cols])
            p = jnp.exp(s - jnp.max(s, axis=-1, keepdims=True))
            den = jnp.sum(p, axis=-1, keepdims=True)
            v = kv_ref[0, :, MEM_W + hd * MEM_HEAD_DIM:MEM_W + (hd + 1) * MEM_HEAD_DIM]
            heads.append((_dot(p.astype(BF16), v) / den).astype(BF16))
        o_c = jnp.concatenate(heads, axis=-1)

        merged = None
        for br, (o, w_ref) in enumerate(((oa_ref[0, rows, :], w_a_ref), (ob_ref[0, rows, :], w_b_ref),
                                         (o_c, w_c_ref))):
            gcols = slice(br * D, (br + 1) * D)
            gate = jax.nn.sigmoid(_dot(h, w_gate_ref[:, gcols]) + b_gate_ref[:, gcols])
            term = gate * _dot(o, w_ref[...])
            merged = term if merged is None else merged + term
        mix = _dot(merged.astype(BF16), w_o_ref[...])
        out_ref[0, rows, :] = x + _rms_norm(mix, g_post_ref[...])


def _mix(x, o_a, o_b, q_c, kv_m, g_pre, g_post, w_gate, b_gate, w_a, w_b, w_c, w_o):
    B, S, D = x.shape
    tm = TOKEN_TILE
    tile = lambda w: pl.BlockSpec((1, tm, w), lambda b, t: (b, t, 0))
    consts = (g_pre, g_post, w_gate, b_gate, w_a, w_b, w_c, w_o)
    return pl.pallas_call(
        _mix_kernel,
        grid=(B, S // tm),
        in_specs=[tile(D), tile(SB_W), tile(DIL_W), tile(MEM_W),
                  pl.BlockSpec((1,) + kv_m.shape[1:], lambda b, t: (b, 0, 0))]
                 + [_const_spec(c.shape) for c in consts],
        out_specs=tile(D),
        out_shape=jax.ShapeDtypeStruct((B, S, D), F32),
        compiler_params=_params("parallel", "parallel"),
        name="mix",
    )(x, o_a, o_b, q_c, kv_m, *consts)


def _ffn_kernel(x_ref, g_pre_ref, g_post_ref, w_gate_ref, w_up_ref, w_out_ref, out_ref):
    for rows in _sub_tiles(x_ref.shape[0]):
        x = x_ref[rows, :]
        h = _rms_norm(x, g_pre_ref[...]).astype(BF16)
        f = jax.nn.silu(_dot(h, w_gate_ref[...])) * _dot(h, w_up_ref[...])
        f = _dot(f.astype(BF16), w_out_ref[...])
        out_ref[rows, :] = x + _rms_norm(f, g_post_ref[...])


def _ffn(x, g_pre, g_post, w_ff, w_out):
    N, D = x.shape
    d_ff = w_out.shape[0]
    tm = FFN_TILE
    half = lambda c: pl.BlockSpec((D, d_ff), lambda t: (0, c), pipeline_mode=pl.Buffered(1))
    return pl.pallas_call(
        _ffn_kernel,
        grid=(N // tm,),
        in_specs=[pl.BlockSpec((tm, D), lambda t: (t, 0)), _const_spec(g_pre.shape), _const_spec(g_post.shape),
                  half(0), half(1), _const_spec(w_out.shape)],
        out_specs=pl.BlockSpec((tm, D), lambda t: (t, 0)),
        out_shape=jax.ShapeDtypeStruct((N, D), F32),
        compiler_params=_params("parallel"),
        name="ffn",
    )(x, g_pre, g_post, w_ff, w_ff, w_out)


def _rope_tables(S):
    half = HEAD_DIM // 2
    inv_freq = ROPE_THETA ** (-jnp.arange(half, dtype=F32) * 2.0 / HEAD_DIM)
    ang = jnp.arange(S, dtype=F32)[:, None] * inv_freq[None, :]
    reps = LANES // half
    return jnp.tile(jnp.cos(ang), (1, reps)), jnp.tile(jnp.sin(ang), (1, reps))


def _rotary_layout(w):
    D = w.shape[0]
    w = w.reshape(D, DIL_HEADS, 2, HEAD_DIM // 2)
    return w.transpose(0, 2, 1, 3).reshape(D, DIL_W)


def _split_w_in(w):
    scale = HEAD_DIM ** -0.5
    w_sb = jnp.concatenate([w[:, :SB_W] * scale, w[:, SB_W:3 * SB_W]], axis=1).astype(BF16)
    w_dil = []
    off = 3 * SB_W
    for _ in DIL_GROUPS:
        q, k, v = (w[:, off + i * DIL_W:off + (i + 1) * DIL_W] for i in range(3))
        w_dil.append(jnp.concatenate([_rotary_layout(q) * scale, _rotary_layout(k), v], axis=1).astype(BF16))
        off += 3 * DIL_W
    w_qc = w[:, off:off + MEM_W].astype(BF16)
    return w_sb, w_qc, w_dil


def kernel(x, mem, g_pre_mix, g_post_mix, g_pre_ffn, g_post_ffn, g_mem, w_in, w_mem_kv, w_br_sb, w_br_dil,
           w_br_mem, w_gate, b_gate, w_o, w_ffn_in, w_ffn_out):
    B, S, D = x.shape
    depth = w_in.shape[0]
    d_ff = w_ffn_out.shape[1]
    cos, sin = _rope_tables(S)
    row = lambda v: v.reshape(1, -1)
    for l in range(depth):
        w_sb, w_qc, w_dil = _split_w_in(w_in[l])
        kv_m = _memkv(mem, row(g_mem[l]), w_mem_kv[l].astype(BF16))
        sb, q_c, *dils = _inproj(x, row(g_pre_mix[l]), cos, sin, w_sb, w_qc, w_dil)
        o_a = _stick(sb)
        o_b = _dilated(dils, S)
        x = _mix(x, o_a, o_b, q_c, kv_m, row(g_pre_mix[l]), row(g_post_mix[l]), w_gate[l].astype(BF16),
                 row(b_gate[l]), w_br_sb[l].astype(BF16), w_br_dil[l].astype(BF16), w_br_mem[l].astype(BF16),
                 w_o[l].astype(BF16))
        x = _ffn(x.reshape(B * S, D), row(g_pre_ffn[l]), row(g_post_ffn[l]), w_ffn_in[l].astype(BF16),
                 w_ffn_out[l].astype(BF16)).reshape(B, S, D)
    return x
```

```python
import functools

import jax
import jax.numpy as jnp
from jax import lax
from jax.experimental import pallas as pl
from jax.experimental.pallas import tpu as pltpu

HEAD_DIM = 64
SB_HEADS = 8
DIL_GROUPS = ((128, 1), (512, 4), (2048, 16))
DIL_HEADS = 4
MEM_HEADS = 4
MEM_HEAD_DIM = 128
N_BRANCHES = 3
BLOCK = 128
ROPE_THETA = 10000.0
NORM_EPS = 1e-6
NEG_INF = -1e30

SB_W = SB_HEADS * HEAD_DIM
DIL_W = DIL_HEADS * HEAD_DIM
MEM_W = MEM_HEADS * MEM_HEAD_DIM
N_GROUPS = len(DIL_GROUPS)

LOG2E = 1.4426950408889634
EXP2_UNDERFLOW = 127.0
QUERY_BLOCKS = 2
HEAD_BLOCKS = 2

LANES = 128
TOKEN_TILE = 1024
SUB_TILE = 512
FFN_TILE = 1024
MEMKV_TILE = 1024
VMEM_LIMIT_BYTES = 56 * 1024 * 1024

BF16 = jnp.bfloat16
F32 = jnp.float32


def _dot(a, b):
    return jnp.dot(a, b, preferred_element_type=F32)


def _dot_nt(a, b):
    return lax.dot_general(a, b, (((1,), (1,)), ((), ())), preferred_element_type=F32)


def _rms_norm(x, g):
    return x * lax.rsqrt(jnp.mean(x * x, axis=-1, keepdims=True) + NORM_EPS) * g


def _const_spec(shape):
    return pl.BlockSpec(shape, lambda *_: (0,) * len(shape), pipeline_mode=pl.Buffered(1))


def _params(*semantics):
    return pltpu.CompilerParams(dimension_semantics=semantics, vmem_limit_bytes=VMEM_LIMIT_BYTES)


def _sub_tiles(tm):
    return [slice(s, s + SUB_TILE) for s in range(0, tm, SUB_TILE)]


def _memkv_kernel(mem_ref, g_ref, w_ref, out_ref):
    h = _rms_norm(mem_ref[...], g_ref[...]).astype(BF16)
    out_ref[...] = _dot(h, w_ref[...]).astype(BF16)


def _memkv(mem, g_mem, w_kv):
    B, M, D = mem.shape
    rows = B * M
    tm = min(MEMKV_TILE, rows)
    return pl.pallas_call(
        _memkv_kernel,
        grid=(rows // tm,),
        in_specs=[pl.BlockSpec((tm, D), lambda t: (t, 0)),
                  _const_spec((1, D)),
                  _const_spec(w_kv.shape)],
        out_specs=pl.BlockSpec((tm, 2 * MEM_W), lambda t: (t, 0)),
        out_shape=jax.ShapeDtypeStruct((rows, 2 * MEM_W), BF16),
        compiler_params=_params("parallel"),
        name="memkv",
    )(mem.reshape(rows, D), g_mem, w_kv).reshape(B, M, 2 * MEM_W)


def _inproj_kernel(x_ref, g_ref, cos_ref, sin_ref, w_sb_ref, w_qc_ref, w_d0_ref, w_d1_ref, w_d2_ref,
                   sb_ref, qc_ref, d0_ref, d1_ref, d2_ref, split_ref):
    def rot(t, cos, sin):
        t1, t2 = t[:, :LANES], t[:, LANES:]
        return jnp.concatenate([t1 * cos - t2 * sin, t2 * cos + t1 * sin], axis=-1)

    for s, rows in enumerate(_sub_tiles(x_ref.shape[1])):
        h = _rms_norm(x_ref[0, rows, :], g_ref[...]).astype(BF16)
        sb = _dot(h, w_sb_ref[...])
        sb_ref[0, rows, :] = jnp.concatenate([sb[:, :SB_W] * LOG2E, sb[:, SB_W:]], axis=-1).astype(BF16)
        qc_ref[0, rows, :] = (_dot(h, w_qc_ref[...]) * (MEM_HEAD_DIM ** -0.5)).astype(BF16)
        cos = cos_ref[rows, :]
        sin = sin_ref[rows, :]
        for (_, dil), w_ref, out_ref in zip(DIL_GROUPS, (w_d0_ref, w_d1_ref, w_d2_ref),
                                            (d0_ref, d1_ref, d2_ref)):
            p = _dot(h, w_ref[...])
            p = jnp.concatenate([rot(p[:, :DIL_W], cos, sin) * LOG2E, rot(p[:, DIL_W:2 * DIL_W], cos, sin),
                                 p[:, 2 * DIL_W:]], axis=-1)
            if dil == 1:
                out_ref[0, 0, rows, :] = p.astype(BF16)
            else:
                nchunk = p.shape[1] // LANES
                per = SUB_TILE // dil
                for c in range(nchunk):
                    split_ref[s, c] = p[:, c * LANES:(c + 1) * LANES]
                for r in range(dil):
                    strided = pl.ds(r, per, stride=dil)
                    out_ref[0, r, s * per:(s + 1) * per, :] = jnp.concatenate(
                        [split_ref[s, c, strided, :] for c in range(nchunk)], axis=-1).astype(BF16)


def _inproj(x, g, cos, sin, w_sb, w_qc, w_dil):
    B, S, D = x.shape
    tm = TOKEN_TILE
    nt = S // tm
    tile = lambda w: pl.BlockSpec((1, tm, w), lambda b, t: (b, t, 0))
    dil_specs = [pl.BlockSpec((1, d, tm // d, 3 * DIL_W), lambda b, t: (b, 0, t, 0)) for _, d in DIL_GROUPS]
    dil_shapes = [jax.ShapeDtypeStruct((B, d, S // d, 3 * DIL_W), BF16) for _, d in DIL_GROUPS]
    return pl.pallas_call(
        _inproj_kernel,
        grid=(B, nt),
        in_specs=[tile(D), _const_spec((1, D)),
                  pl.BlockSpec((tm, LANES), lambda b, t: (t, 0)),
                  pl.BlockSpec((tm, LANES), lambda b, t: (t, 0)),
                  _const_spec(w_sb.shape), _const_spec(w_qc.shape)] + [_const_spec(w.shape) for w in w_dil],
        out_specs=[tile(3 * SB_W), tile(MEM_W)] + dil_specs,
        out_shape=[jax.ShapeDtypeStruct((B, S, 3 * SB_W), BF16),
                   jax.ShapeDtypeStruct((B, S, MEM_W), BF16)] + dil_shapes,
        scratch_shapes=[pltpu.VMEM((tm // SUB_TILE, 3 * DIL_W // LANES, SUB_TILE, LANES), F32)],
        compiler_params=_params("parallel", "parallel"),
        name="inproj",
    )(x, g, cos, sin, w_sb, w_qc, *w_dil)


def _stick_kernel(q_ref, k_ref, v_ref, o_ref, qq_ref, tri_ref, carry_ref, acc_ref):
    nblk = q_ref.shape[1] // BLOCK
    npair = q_ref.shape[2] // LANES
    pairs = range(npair)
    lane = lax.broadcasted_iota(jnp.int32, (BLOCK, LANES), 1)
    row = lax.broadcasted_iota(jnp.int32, (BLOCK, LANES), 0)
    first_head = lane < HEAD_DIM
    causal = jnp.concatenate([lane < row, lane < row], axis=0)
    zero = jnp.zeros((), BF16)
    tri = jnp.concatenate([(row >= lane), jnp.ones((BLOCK, LANES), jnp.bool_)], axis=1)
    tri = jnp.where(tri, 1.0, 0.0).astype(BF16)
    tri_ref[...] = jnp.concatenate([tri, tri], axis=0)

    def blocks(groups):
        cols = [slice(p * LANES, (p + 1) * LANES) for p in pairs]
        krows = lambda j: pl.ds(pl.multiple_of(j * BLOCK, BLOCK), BLOCK)
        zs = [[[_dot_nt(qq_ref[u, p], k_ref[0, krows(j), cols[p]]) for p in pairs] for j in js]
              for u, js, _ in groups]
        sums = [[[None] * npair for _ in js] for _, js, _ in groups]
        for g, (u, js, diagonal_first) in enumerate(groups):
            for e in range(len(js)):
                for p in pairs:
                    z = zs[g][e][p]
                    soft = jnp.maximum(z, 0.0) + jnp.log(1.0 + jnp.exp2(-jnp.abs(z))) * LOG2E
                    if diagonal_first and e == 0:
                        soft = jnp.where(causal, soft, 0.0)
                    hi = soft.astype(BF16)
                    lo = (soft - hi.astype(F32)).astype(BF16)
                    sums[g][e][p] = _dot(jnp.concatenate([hi, lo], axis=1), tri_ref[...])
        leasts = []
        for g, (u, js, diagonal_first) in enumerate(groups):
            least = None
            for p in pairs:
                carry = None if diagonal_first else carry_ref[u, p]
                pws, v2s = [], []
                for e, j in enumerate(js):
                    after, total = sums[g][e][p][:, :LANES], sums[g][e][p][:, LANES:]
                    if carry is None:
                        carry = total
                    else:
                        after = after + carry
                        carry = carry + total
                    weight = jnp.exp2(zs[g][e][p] - after)
                    if diagonal_first and e == 0:
                        weight = jnp.where(causal, weight, 0.0)
                    weight = weight.astype(BF16)
                    pws.append(jnp.concatenate([weight[:BLOCK], weight[BLOCK:]], axis=1))
                    v = v_ref[0, krows(j), cols[p]]
                    v2s.append(jnp.concatenate([jnp.where(first_head, v, zero), jnp.where(first_head, zero, v)],
                                               axis=0))
                carry_ref[u, p] = carry
                least = carry if least is None else jnp.minimum(least, carry)
                out = _dot(jnp.concatenate(pws, axis=1), jnp.concatenate(v2s, axis=0))
                if diagonal_first:
                    acc_ref[u, p] = out
                else:
                    acc_ref[u, p] += out
            leasts.append(jnp.min(least))
        return leasts

    def q_blocks(first):
        slots = range(QUERY_BLOCKS)
        qrows = [pl.ds(pl.multiple_of((first + u) * BLOCK, BLOCK), BLOCK) for u in slots]
        for u in slots:
            for p in pairs:
                q = q_ref[0, qrows[u], p * LANES:(p + 1) * LANES]
                qq_ref[u, p] = jnp.concatenate([jnp.where(first_head, q, zero), jnp.where(first_head, zero, q)],
                                               axis=0)
        depth = [HEAD_BLOCKS if not isinstance(first, int) else min(HEAD_BLOCKS, first + u + 1) for u in slots]
        leasts = blocks([(u, [first + u - e for e in range(depth[u])], True) for u in slots])
        for u in slots:
            i = first + u

            def more(state, i=i):
                t, least_carry = state
                return jnp.logical_and(t < i, least_carry < EXP2_UNDERFLOW)

            def kv_step(state, i=i, u=u):
                t, _ = state
                return t + 1, blocks([(u, [i - 1 - t], False)])[0]

            lax.while_loop(more, kv_step, (jnp.int32(depth[u] - 1), leasts[u]))
            for p in pairs:
                o_ref[0, qrows[u], p * LANES:(p + 1) * LANES] = acc_ref[u, p].astype(o_ref.dtype)

    assert nblk % QUERY_BLOCKS == 0 and HEAD_BLOCKS <= QUERY_BLOCKS + 1
    q_blocks(0)

    def rest(t, _):
        q_blocks(t * QUERY_BLOCKS)
        return 0

    lax.fori_loop(1, nblk // QUERY_BLOCKS, rest, 0)


def _stick(sb):
    B, S, _ = sb.shape
    npair = SB_W // LANES
    spec = lambda c: pl.BlockSpec((1, S, SB_W), lambda b: (b, 0, c))
    return pl.pallas_call(
        _stick_kernel,
        grid=(B,),
        in_specs=[spec(0), spec(1), spec(2)],
        out_specs=spec(0),
        out_shape=jax.ShapeDtypeStruct((B, S, SB_W), BF16),
        scratch_shapes=[pltpu.VMEM((QUERY_BLOCKS, npair, 2 * BLOCK, LANES), BF16),
                        pltpu.VMEM((2 * BLOCK, 2 * LANES), BF16),
                        pltpu.VMEM((QUERY_BLOCKS, npair, 2 * BLOCK, LANES), F32),
                        pltpu.VMEM((QUERY_BLOCKS, npair, BLOCK, LANES), F32)],
        compiler_params=_params("parallel"),
        name="stick",
    )(sb, sb, sb)


def _dilated_kernel(d0_ref, d1_ref, d2_ref, o_ref, cap_ref, og_ref, lse_ref):
    S = o_ref.shape[1]
    nchunk = DIL_W // LANES
    span = BLOCK
    assert all(window // dil == span for window, dil in DIL_GROUPS)
    lane_qk = lax.broadcasted_iota(jnp.int32, (BLOCK, DIL_W), 1)
    qk_head = (lane_qk % LANES) // (HEAD_DIM // 2)
    first_head = lax.broadcasted_iota(jnp.int32, (BLOCK, LANES), 1) < HEAD_DIM
    zero = jnp.zeros((), BF16)

    qi = lax.broadcasted_iota(jnp.int32, (BLOCK, 2 * BLOCK), 0)
    kj = lax.broadcasted_iota(jnp.int32, (BLOCK, 2 * BLOCK), 1)
    for a in range(2):
        dist = qi + a * BLOCK - kj
        cap_ref[a] = jnp.where((dist >= 0) & (dist <= span), float(jnp.finfo(F32).max), NEG_INF)

    def key_window(ref, i):
        L = ref.shape[2]
        nk = min(2 * BLOCK, L)
        first = jnp.maximum(i - 1, 0) if nk < L else 0
        return first, nk

    def scores(ref, r, i):
        first, nk = key_window(ref, i)
        q = ref[0, r, pl.ds(i * BLOCK, BLOCK), 0:DIL_W]
        kk = ref[0, r, pl.ds(first * BLOCK, nk), DIL_W:2 * DIL_W]
        qs = jnp.concatenate([jnp.where(qk_head == h, q, zero) for h in range(DIL_HEADS)], axis=0)
        return _dot_nt(qs, kk)

    def attend(g, ref, dil, r, i, s):
        first, nk = key_window(ref, i)
        vv = ref[0, r, pl.ds(first * BLOCK, nk), 2 * DIL_W:3 * DIL_W]
        cap = cap_ref[i - first, :, 0:nk]
        s = jnp.minimum(s, jnp.concatenate([cap] * DIL_HEADS, axis=0))
        m = jnp.max(s, axis=-1, keepdims=True)
        p = jnp.exp2((s - m).astype(BF16))
        ones = jnp.ones((nk, LANES), BF16)
        rows = pl.ds(r + dil * BLOCK * i, BLOCK, stride=dil) if dil > 1 else pl.ds(i * BLOCK, BLOCK)
        per = LANES // HEAD_DIM
        for c in range(nchunk):
            top = slice(per * c * BLOCK, (per * c + 1) * BLOCK)
            bot = slice((per * c + 1) * BLOCK, (per * c + 2) * BLOCK)
            pv = _dot(p[per * c * BLOCK:per * (c + 1) * BLOCK],
                      jnp.concatenate([vv[:, c * LANES:(c + 1) * LANES], ones], axis=1))
            pv, den = pv[:, :LANES], pv[:, LANES:]
            den_c = jnp.where(first_head, den[:BLOCK], den[BLOCK:])
            m_c = jnp.where(first_head, m[top], m[bot])
            og_ref[g, c, rows, :] = jnp.where(first_head, pv[:BLOCK], pv[BLOCK:]) * (1.0 / den_c)
            lse_ref[g, c, rows, :] = m_c + jnp.log(den_c) * LOG2E

    side = 4
    for g, ((_, dil), ref) in enumerate(zip(DIL_GROUPS, (d0_ref, d1_ref, d2_ref))):
        nb = (S // dil) // BLOCK
        units = dil * nb
        assert units % side == 0

        def unit_group(t, _, g=g, ref=ref, dil=dil, nb=nb):
            where = [((side * t + u) // nb, (side * t + u) % nb) for u in range(side)]
            ss = [scores(ref, r, i) for r, i in where]
            for (r, i), s in zip(where, ss):
                attend(g, ref, dil, r, i, s)
            return 0

        lax.fori_loop(0, units // side, unit_group, 0)

    chunk = 2 * BLOCK

    def merge(t, _):
        rows = pl.ds(t * chunk, chunk)
        for c in range(nchunk):
            lses = [lse_ref[g, c, rows, :] for g in range(N_GROUPS)]
            top = functools.reduce(jnp.maximum, lses)
            ws = [jnp.exp2(l - top) for l in lses]
            num = functools.reduce(lambda a, b: a + b, [w * og_ref[g, c, rows, :] for g, w in enumerate(ws)])
            den = functools.reduce(lambda a, b: a + b, ws)
            o_ref[0, rows, c * LANES:(c + 1) * LANES] = (num / den).astype(o_ref.dtype)
        return 0

    lax.fori_loop(0, S // chunk, merge, 0)


def _dilated(dils, S):
    B = dils[0].shape[0]
    return pl.pallas_call(
        _dilated_kernel,
        grid=(B,),
        in_specs=[pl.BlockSpec((1,) + d.shape[1:], lambda b: (b, 0, 0, 0)) for d in dils],
        out_specs=pl.BlockSpec((1, S, DIL_W), lambda b: (b, 0, 0)),
        out_shape=jax.ShapeDtypeStruct((B, S, DIL_W), BF16),
        scratch_shapes=[pltpu.VMEM((2, BLOCK, 2 * BLOCK), F32)]
                       + [pltpu.VMEM((N_GROUPS, DIL_W // LANES, S, LANES), F32)] * 2,
        compiler_params=_params("parallel"),
        name="dilated",
    )(*dils)


def _mix_kernel(x_ref, oa_ref, ob_ref, qc_ref, kv_ref, g_pre_ref, g_post_ref, w_gate_ref, b_gate_ref,
                w_a_ref, w_b_ref, w_c_ref, w_o_ref, out_ref):
    D = x_ref.shape[-1]
    for rows in _sub_tiles(x_ref.shape[1]):
        x = x_ref[0, rows, :]
        h = _rms_norm(x, g_pre_ref[...]).astype(BF16)

        heads = []
        for hd in range(MEM_HEADS):
            cols = slice(hd * MEM_HEAD_DIM, (hd + 1) * MEM_HEAD_DIM)
            s = _dot_nt(qc_ref[0, rows, cols], kv_ref[0, :, cols])
            p = jnp.exp(s - jnp.max(s, axis=-1, keepdims=True))
            den = jnp.sum(p, axis=-1, keepdims=True)
            v = kv_ref[0, :, MEM_W + hd * MEM_HEAD_DIM:MEM_W + (hd + 1) * MEM_HEAD_DIM]
            heads.append((_dot(p.astype(BF16), v) / den).astype(BF16))
        o_c = jnp.concatenate(heads, axis=-1)

        merged = None
        for br, (o, w_ref) in enumerate(((oa_ref[0, rows, :], w_a_ref), (ob_ref[0, rows, :], w_b_ref),
                                         (o_c, w_c_ref))):
            gcols = slice(br * D, (br + 1) * D)
            gate = jax.nn.sigmoid(_dot(h, w_gate_ref[:, gcols]) + b_gate_ref[:, gcols])
            term = gate * _dot(o, w_ref[...])
            merged = term if merged is None else merged + term
        mix = _dot(merged.astype(BF16), w_o_ref[...])
        out_ref[0, rows, :] = x + _rms_norm(mix, g_post_ref[...])


def _mix(x, o_a, o_b, q_c, kv_m, g_pre, g_post, w_gate, b_gate, w_a, w_b, w_c, w_o):
    B, S, D = x.shape
    tm = TOKEN_TILE
    tile = lambda w: pl.BlockSpec((1, tm, w), lambda b, t: (b, t, 0))
    consts = (g_pre, g_post, w_gate, b_gate, w_a, w_b, w_c, w_o)
    return pl.pallas_call(
        _mix_kernel,
        grid=(B, S // tm),
        in_specs=[tile(D), tile(SB_W), tile(DIL_W), tile(MEM_W),
                  pl.BlockSpec((1,) + kv_m.shape[1:], lambda b, t: (b, 0, 0))]
                 + [_const_spec(c.shape) for c in consts],
        out_specs=tile(D),
        out_shape=jax.ShapeDtypeStruct((B, S, D), F32),
        compiler_params=_params("parallel", "parallel"),
        name="mix",
    )(x, o_a, o_b, q_c, kv_m, *consts)


def _ffn_kernel(x_ref, g_pre_ref, g_post_ref, w_gate_ref, w_up_ref, w_out_ref, out_ref):
    for rows in _sub_tiles(x_ref.shape[0]):
        x = x_ref[rows, :]
        h = _rms_norm(x, g_pre_ref[...]).astype(BF16)
        f = jax.nn.silu(_dot(h, w_gate_ref[...])) * _dot(h, w_up_ref[...])
        f = _dot(f.astype(BF16), w_out_ref[...])
        out_ref[rows, :] = x + _rms_norm(f, g_post_ref[...])


def _ffn(x, g_pre, g_post, w_ff, w_out):
    N, D = x.shape
    d_ff = w_out.shape[0]
    tm = FFN_TILE
    half = lambda c: pl.BlockSpec((D, d_ff), lambda t: (0, c), pipeline_mode=pl.Buffered(1))
    return pl.pallas_call(
        _ffn_kernel,
        grid=(N // tm,),
        in_specs=[pl.BlockSpec((tm, D), lambda t: (t, 0)), _const_spec(g_pre.shape), _const_spec(g_post.shape),
                  half(0), half(1), _const_spec(w_out.shape)],
        out_specs=pl.BlockSpec((tm, D), lambda t: (t, 0)),
        out_shape=jax.ShapeDtypeStruct((N, D), F32),
        compiler_params=_params("parallel"),
        name="ffn",
    )(x, g_pre, g_post, w_ff, w_ff, w_out)


def _rope_tables(S):
    half = HEAD_DIM // 2
    inv_freq = ROPE_THETA ** (-jnp.arange(half, dtype=F32) * 2.0 / HEAD_DIM)
    ang = jnp.arange(S, dtype=F32)[:, None] * inv_freq[None, :]
    reps = LANES // half
    return jnp.tile(jnp.cos(ang), (1, reps)), jnp.tile(jnp.sin(ang), (1, reps))


def _rotary_layout(w):
    D = w.shape[0]
    w = w.reshape(D, DIL_HEADS, 2, HEAD_DIM // 2)
    return w.transpose(0, 2, 1, 3).reshape(D, DIL_W)


def _split_w_in(w):
    scale = HEAD_DIM ** -0.5
    w_sb = jnp.concatenate([w[:, :SB_W] * scale, w[:, SB_W:3 * SB_W]], axis=1).astype(BF16)
    w_dil = []
    off = 3 * SB_W
    for _ in DIL_GROUPS:
        q, k, v = (w[:, off + i * DIL_W:off + (i + 1) * DIL_W] for i in range(3))
        w_dil.append(jnp.concatenate([_rotary_layout(q) * scale, _rotary_layout(k), v], axis=1).astype(BF16))
        off += 3 * DIL_W
    w_qc = w[:, off:off + MEM_W].astype(BF16)
    return w_sb, w_qc, w_dil


def kernel(x, mem, g_pre_mix, g_post_mix, g_pre_ffn, g_post_ffn, g_mem, w_in, w_mem_kv, w_br_sb, w_br_dil,
           w_br_mem, w_gate, b_gate, w_o, w_ffn_in, w_ffn_out):
    B, S, D = x.shape
    depth = w_in.shape[0]
    d_ff = w_ffn_out.shape[1]
    cos, sin = _rope_tables(S)
    row = lambda v: v.reshape(1, -1)
    for l in range(depth):
        w_sb, w_qc, w_dil = _split_w_in(w_in[l])
        kv_m = _memkv(mem, row(g_mem[l]), w_mem_kv[l].astype(BF16))
        sb, q_c, *dils = _inproj(x, row(g_pre_mix[l]), cos, sin, w_sb, w_qc, w_dil)
        o_a = _stick(sb)
        o_b = _dilated(dils, S)
        x = _mix(x, o_a, o_b, q_c, kv_m, row(g_pre_mix[l]), row(g_post_mix[l]), w_gate[l].astype(BF16),
                 row(b_gate[l]), w_br_sb[l].astype(BF16), w_br_dil[l].astype(BF16), w_br_mem[l].astype(BF16),
                 w_o[l].astype(BF16))
        x = _ffn(x.reshape(B * S, D), row(g_pre_ffn[l]), row(g_post_ffn[l]), w_ffn_in[l].astype(BF16),
                 w_ffn_out[l].astype(BF16)).reshape(B, S, D)
    return x
```

```python
import functools

import jax
import jax.numpy as jnp
from jax import lax
from jax.experimental import pallas as pl
from jax.experimental.pallas import tpu as pltpu

HEAD_DIM = 64
SB_HEADS = 8
DIL_GROUPS = ((128, 1), (512, 4), (2048, 16))
DIL_HEADS = 4
MEM_HEADS = 4
MEM_HEAD_DIM = 128
N_BRANCHES = 3
BLOCK = 128
ROPE_THETA = 10000.0
NORM_EPS = 1e-6
NEG_INF = -1e30

SB_W = SB_HEADS * HEAD_DIM
DIL_W = DIL_HEADS * HEAD_DIM
MEM_W = MEM_HEADS * MEM_HEAD_DIM
N_GROUPS = len(DIL_GROUPS)

LOG2E = 1.4426950408889634
EXP2_UNDERFLOW = 127.0
QUERY_BLOCKS = 4
HEAD_BLOCKS = 2

LANES = 128
TOKEN_TILE = 1024
SUB_TILE = 512
FFN_TILE = 1024
MEMKV_TILE = 1024
VMEM_LIMIT_BYTES = 56 * 1024 * 1024

BF16 = jnp.bfloat16
F32 = jnp.float32


def _dot(a, b):
    return jnp.dot(a, b, preferred_element_type=F32)


def _dot_nt(a, b):
    return lax.dot_general(a, b, (((1,), (1,)), ((), ())), preferred_element_type=F32)


def _rms_norm(x, g):
    return x * lax.rsqrt(jnp.mean(x * x, axis=-1, keepdims=True) + NORM_EPS) * g


def _const_spec(shape):
    return pl.BlockSpec(shape, lambda *_: (0,) * len(shape), pipeline_mode=pl.Buffered(1))


def _params(*semantics):
    return pltpu.CompilerParams(dimension_semantics=semantics, vmem_limit_bytes=VMEM_LIMIT_BYTES)


def _sub_tiles(tm):
    return [slice(s, s + SUB_TILE) for s in range(0, tm, SUB_TILE)]


def _memkv_kernel(mem_ref, g_ref, w_ref, out_ref):
    h = _rms_norm(mem_ref[...], g_ref[...]).astype(BF16)
    out_ref[...] = _dot(h, w_ref[...]).astype(BF16)


def _memkv(mem, g_mem, w_kv):
    B, M, D = mem.shape
    rows = B * M
    tm = min(MEMKV_TILE, rows)
    return pl.pallas_call(
        _memkv_kernel,
        grid=(rows // tm,),
        in_specs=[pl.BlockSpec((tm, D), lambda t: (t, 0)),
                  _const_spec((1, D)),
                  _const_spec(w_kv.shape)],
        out_specs=pl.BlockSpec((tm, 2 * MEM_W), lambda t: (t, 0)),
        out_shape=jax.ShapeDtypeStruct((rows, 2 * MEM_W), BF16),
        compiler_params=_params("parallel"),
        name="memkv",
    )(mem.reshape(rows, D), g_mem, w_kv).reshape(B, M, 2 * MEM_W)


def _inproj_kernel(x_ref, g_ref, cos_ref, sin_ref, w_sb_ref, w_qc_ref, w_d0_ref, w_d1_ref, w_d2_ref,
                   sb_ref, qc_ref, d0_ref, d1_ref, d2_ref, split_ref):
    def rot(t, cos, sin):
        t1, t2 = t[:, :LANES], t[:, LANES:]
        return jnp.concatenate([t1 * cos - t2 * sin, t2 * cos + t1 * sin], axis=-1)

    for s, rows in enumerate(_sub_tiles(x_ref.shape[1])):
        h = _rms_norm(x_ref[0, rows, :], g_ref[...]).astype(BF16)
        cos = cos_ref[rows, :]
        sin = sin_ref[rows, :]
        for (_, dil), w_ref, out_ref in reversed(list(zip(DIL_GROUPS, (w_d0_ref, w_d1_ref, w_d2_ref),
                                                          (d0_ref, d1_ref, d2_ref)))):
            p = _dot(h, w_ref[...])
            p = jnp.concatenate([rot(p[:, :DIL_W], cos, sin) * LOG2E, rot(p[:, DIL_W:2 * DIL_W], cos, sin),
                                 p[:, 2 * DIL_W:]], axis=-1)
            if dil == 1:
                out_ref[0, 0, rows, :] = p.astype(BF16)
            else:
                nchunk = p.shape[1] // LANES
                per = SUB_TILE // dil
                for c in range(nchunk):
                    split_ref[s, c] = p[:, c * LANES:(c + 1) * LANES]
                for r in range(dil):
                    strided = pl.ds(r, per, stride=dil)
                    out_ref[0, r, s * per:(s + 1) * per, :] = jnp.concatenate(
                        [split_ref[s, c, strided, :] for c in range(nchunk)], axis=-1).astype(BF16)
        qc_ref[0, rows, :] = (_dot(h, w_qc_ref[...]) * (MEM_HEAD_DIM ** -0.5)).astype(BF16)
        sb = _dot(h, w_sb_ref[...])
        sb_ref[0, rows, :] = jnp.concatenate([sb[:, :SB_W] * LOG2E, sb[:, SB_W:]], axis=-1).astype(BF16)


def _inproj(x, g, cos, sin, w_sb, w_qc, w_dil):
    B, S, D = x.shape
    tm = TOKEN_TILE
    nt = S // tm
    tile = lambda w: pl.BlockSpec((1, tm, w), lambda b, t: (b, t, 0))
    dil_specs = [pl.BlockSpec((1, d, tm // d, 3 * DIL_W), lambda b, t: (b, 0, t, 0)) for _, d in DIL_GROUPS]
    dil_shapes = [jax.ShapeDtypeStruct((B, d, S // d, 3 * DIL_W), BF16) for _, d in DIL_GROUPS]
    return pl.pallas_call(
        _inproj_kernel,
        grid=(B, nt),
        in_specs=[tile(D), _const_spec((1, D)),
                  pl.BlockSpec((tm, LANES), lambda b, t: (t, 0)),
                  pl.BlockSpec((tm, LANES), lambda b, t: (t, 0)),
                  _const_spec(w_sb.shape), _const_spec(w_qc.shape)] + [_const_spec(w.shape) for w in w_dil],
        out_specs=[tile(3 * SB_W), tile(MEM_W)] + dil_specs,
        out_shape=[jax.ShapeDtypeStruct((B, S, 3 * SB_W), BF16),
                   jax.ShapeDtypeStruct((B, S, MEM_W), BF16)] + dil_shapes,
        scratch_shapes=[pltpu.VMEM((tm // SUB_TILE, 3 * DIL_W // LANES, SUB_TILE, LANES), F32)],
        compiler_params=_params("parallel", "parallel"),
        name="inproj",
    )(x, g, cos, sin, w_sb, w_qc, *w_dil)


def _stick_kernel(q_ref, k_ref, v_ref, o_ref, qq_ref, tri_ref, carry_ref, acc_ref):
    nblk = q_ref.shape[1] // BLOCK
    npair = q_ref.shape[2] // LANES
    pairs = range(npair)
    lane = lax.broadcasted_iota(jnp.int32, (BLOCK, LANES), 1)
    row = lax.broadcasted_iota(jnp.int32, (BLOCK, LANES), 0)
    first_head = lane < HEAD_DIM
    causal = jnp.concatenate([lane < row, lane < row], axis=0)
    zero = jnp.zeros((), BF16)
    tri = jnp.concatenate([(row >= lane), jnp.ones((BLOCK, LANES), jnp.bool_)], axis=1)
    tri = jnp.where(tri, 1.0, 0.0).astype(BF16)
    tri_ref[...] = jnp.concatenate([tri, tri], axis=0)

    def blocks(groups):
        cols = [slice(p * LANES, (p + 1) * LANES) for p in pairs]
        krows = lambda j: pl.ds(pl.multiple_of(j * BLOCK, BLOCK), BLOCK)
        zs = [[[_dot_nt(qq_ref[u, p], k_ref[0, krows(j), cols[p]]) for p in pairs] for j in js]
              for u, js, _ in groups]
        for g, (_, _, diagonal_first) in enumerate(groups):
            if diagonal_first:
                zs[g][0] = [jnp.where(causal, z, NEG_INF) for z in zs[g][0]]
        sums = [[[None] * npair for _ in js] for _, js, _ in groups]
        for g, (u, js, diagonal_first) in enumerate(groups):
            for e in range(len(js)):
                for p in pairs:
                    z = zs[g][e][p]
                    soft = jnp.maximum(z, 0.0) + jnp.log(1.0 + jnp.exp2(-jnp.abs(z))) * LOG2E
                    hi = soft.astype(BF16)
                    lo = (soft - hi.astype(F32)).astype(BF16)
                    sums[g][e][p] = _dot(jnp.concatenate([hi, lo], axis=1), tri_ref[...])
        leasts = []
        for g, (u, js, diagonal_first) in enumerate(groups):
            least = None
            for p in pairs:
                carry = None if diagonal_first else carry_ref[u, p]
                pws, v2s = [], []
                for e, j in enumerate(js):
                    after, total = sums[g][e][p][:, :LANES], sums[g][e][p][:, LANES:]
                    if carry is None:
                        carry = total
                    else:
                        after = after + carry
                        carry = carry + total
                    weight = jnp.exp2(zs[g][e][p] - after).astype(BF16)
                    pws.append(jnp.concatenate([weight[:BLOCK], weight[BLOCK:]], axis=1))
                    v = v_ref[0, krows(j), cols[p]]
                    v2s.append(jnp.concatenate([jnp.where(first_head, v, zero), jnp.where(first_head, zero, v)],
                                               axis=0))
                carry_ref[u, p] = carry
                least = carry if least is None else jnp.minimum(least, carry)
                out = _dot(jnp.concatenate(pws, axis=1), jnp.concatenate(v2s, axis=0))
                if diagonal_first:
                    acc_ref[u, p] = out
                else:
                    acc_ref[u, p] += out
            leasts.append(jnp.min(least))
        return leasts

    def q_blocks(first):
        slots = range(QUERY_BLOCKS)
        qrows = [pl.ds(pl.multiple_of((first + u) * BLOCK, BLOCK), BLOCK) for u in slots]
        for u in slots:
            for p in pairs:
                q = q_ref[0, qrows[u], p * LANES:(p + 1) * LANES]
                qq_ref[u, p] = jnp.concatenate([jnp.where(first_head, q, zero), jnp.where(first_head, zero, q)],
                                               axis=0)
        depth = [HEAD_BLOCKS if not isinstance(first, int) else min(HEAD_BLOCKS, first + u + 1) for u in slots]
        leasts = blocks([(u, [first + u - e for e in range(depth[u])], True) for u in slots])
        for u in slots:
            i = first + u

            def more(state, i=i):
                t, least_carry = state
                return jnp.logical_and(t < i, least_carry < EXP2_UNDERFLOW)

            def kv_step(state, i=i, u=u):
                t, _ = state
                return t + 1, blocks([(u, [i - 1 - t], False)])[0]

            lax.while_loop(more, kv_step, (jnp.int32(depth[u] - 1), leasts[u]))
            for p in pairs:
                o_ref[0, qrows[u], p * LANES:(p + 1) * LANES] = acc_ref[u, p].astype(o_ref.dtype)

    assert nblk % QUERY_BLOCKS == 0 and HEAD_BLOCKS <= QUERY_BLOCKS + 1
    q_blocks(0)

    def rest(t, _):
        q_blocks(t * QUERY_BLOCKS)
        return 0

    lax.fori_loop(1, nblk // QUERY_BLOCKS, rest, 0)


def _stick(sb):
    B, S, _ = sb.shape
    npair = SB_W // LANES
    spec = lambda c: pl.BlockSpec((1, S, SB_W), lambda b: (b, 0, c))
    return pl.pallas_call(
        _stick_kernel,
        grid=(B,),
        in_specs=[spec(0), spec(1), spec(2)],
        out_specs=spec(0),
        out_shape=jax.ShapeDtypeStruct((B, S, SB_W), BF16),
        scratch_shapes=[pltpu.VMEM((QUERY_BLOCKS, npair, 2 * BLOCK, LANES), BF16),
                        pltpu.VMEM((2 * BLOCK, 2 * LANES), BF16),
                        pltpu.VMEM((QUERY_BLOCKS, npair, 2 * BLOCK, LANES), F32),
                        pltpu.VMEM((QUERY_BLOCKS, npair, BLOCK, LANES), F32)],
        compiler_params=_params("parallel"),
        name="stick",
    )(sb, sb, sb)


def _dilated_kernel(d0_ref, d1_ref, d2_ref, o_ref, cap_ref, og_ref, lse_ref):
    S = o_ref.shape[1]
    nchunk = DIL_W // LANES
    span = BLOCK
    assert all(window // dil == span for window, dil in DIL_GROUPS)
    lane_qk = lax.broadcasted_iota(jnp.int32, (BLOCK, DIL_W), 1)
    qk_head = (lane_qk % LANES) // (HEAD_DIM // 2)
    first_head = lax.broadcasted_iota(jnp.int32, (BLOCK, LANES), 1) < HEAD_DIM
    zero = jnp.zeros((), BF16)

    qi = lax.broadcasted_iota(jnp.int32, (BLOCK, 2 * BLOCK), 0)
    kj = lax.broadcasted_iota(jnp.int32, (BLOCK, 2 * BLOCK), 1)
    for a in range(2):
        dist = qi + a * BLOCK - kj
        cap_ref[a] = jnp.where((dist >= 0) & (dist <= span), float(jnp.finfo(F32).max), NEG_INF)

    def key_window(ref, i):
        L = ref.shape[2]
        nk = min(2 * BLOCK, L)
        first = jnp.maximum(i - 1, 0) if nk < L else 0
        return first, nk

    def scores(ref, r, i):
        first, nk = key_window(ref, i)
        q = ref[0, r, pl.ds(i * BLOCK, BLOCK), 0:DIL_W]
        kk = ref[0, r, pl.ds(first * BLOCK, nk), DIL_W:2 * DIL_W]
        qs = jnp.concatenate([jnp.where(qk_head == h, q, zero) for h in range(DIL_HEADS)], axis=0)
        return _dot_nt(qs, kk)

    def attend(g, ref, dil, r, i, s):
        first, nk = key_window(ref, i)
        vv = ref[0, r, pl.ds(first * BLOCK, nk), 2 * DIL_W:3 * DIL_W]
        cap = cap_ref[i - first, :, 0:nk]
        s = jnp.minimum(s, jnp.concatenate([cap] * DIL_HEADS, axis=0))
        m = jnp.max(s, axis=-1, keepdims=True)
        p = jnp.exp2((s - m).astype(BF16))
        ones = jnp.ones((nk, LANES), BF16)
        rows = pl.ds(r + dil * BLOCK * i, BLOCK, stride=dil) if dil > 1 else pl.ds(i * BLOCK, BLOCK)
        per = LANES // HEAD_DIM
        for c in range(nchunk):
            top = slice(per * c * BLOCK, (per * c + 1) * BLOCK)
            bot = slice((per * c + 1) * BLOCK, (per * c + 2) * BLOCK)
            pv = _dot(p[per * c * BLOCK:per * (c + 1) * BLOCK],
                      jnp.concatenate([vv[:, c * LANES:(c + 1) * LANES], ones], axis=1))
            pv, den = pv[:, :LANES], pv[:, LANES:]
            den_c = jnp.where(first_head, den[:BLOCK], den[BLOCK:])
            m_c = jnp.where(first_head, m[top], m[bot])
            og_ref[g, c, rows, :] = jnp.where(first_head, pv[:BLOCK], pv[BLOCK:]) * (1.0 / den_c)
            lse_ref[g, c, rows, :] = m_c + jnp.log(den_c) * LOG2E

    side = 8
    for g, ((_, dil), ref) in enumerate(zip(DIL_GROUPS, (d0_ref, d1_ref, d2_ref))):
        nb = (S // dil) // BLOCK
        units = dil * nb
        assert units % side == 0

        def unit_group(t, _, g=g, ref=ref, dil=dil, nb=nb):
            where = [((side * t + u) // nb, (side * t + u) % nb) for u in range(side)]
            ss = [scores(ref, r, i) for r, i in where]
            for (r, i), s in zip(where, ss):
                attend(g, ref, dil, r, i, s)
            return 0

        lax.fori_loop(0, units // side, unit_group, 0)

    chunk = 2 * BLOCK

    def merge(t, _):
        rows = pl.ds(t * chunk, chunk)
        for c in range(nchunk):
            lses = [lse_ref[g, c, rows, :] for g in range(N_GROUPS)]
            top = functools.reduce(jnp.maximum, lses)
            ws = [jnp.exp2(l - top) for l in lses]
            num = functools.reduce(lambda a, b: a + b, [w * og_ref[g, c, rows, :] for g, w in enumerate(ws)])
            den = functools.reduce(lambda a, b: a + b, ws)
            o_ref[0, rows, c * LANES:(c + 1) * LANES] = (num / den).astype(o_ref.dtype)
        return 0

    lax.fori_loop(0, S // chunk, merge, 0)


def _dilated(dils, S):
    B = dils[0].shape[0]
    return pl.pallas_call(
        _dilated_kernel,
        grid=(B,),
        in_specs=[pl.BlockSpec((1,) + d.shape[1:], lambda b: (b, 0, 0, 0)) for d in dils],
        out_specs=pl.BlockSpec((1, S, DIL_W), lambda b: (b, 0, 0)),
        out_shape=jax.ShapeDtypeStruct((B, S, DIL_W), BF16),
        scratch_shapes=[pltpu.VMEM((2, BLOCK, 2 * BLOCK), F32)]
                       + [pltpu.VMEM((N_GROUPS, DIL_W // LANES, S, LANES), F32)] * 2,
        compiler_params=_params("parallel"),
        name="dilated",
    )(*dils)


def _mix_kernel(x_ref, oa_ref, ob_ref, qc_ref, kv_ref, g_pre_ref, g_post_ref, w_gate_ref, b_gate_ref,
                w_a_ref, w_b_ref, w_c_ref, w_o_ref, out_ref):
    D = x_ref.shape[-1]
    for rows in _sub_tiles(x_ref.shape[1]):
        x = x_ref[0, rows, :]
        h = _rms_norm(x, g_pre_ref[...]).astype(BF16)

        heads = []
        for hd in range(MEM_HEADS):
            cols = slice(hd * MEM_HEAD_DIM, (hd + 1) * MEM_HEAD_DIM)
            s = _dot_nt(qc_ref[0, rows, cols], kv_ref[0, :, cols])
            p = jnp.exp(s - jnp.max(s, axis=-1, keepdims=True))
            den = jnp.sum(p, axis=-1, keepdims=True)
            v = kv_ref[0, :, MEM_W + hd * MEM_HEAD_DIM:MEM_W + (hd + 1) * MEM_HEAD_DIM]
            heads.append((_dot(p.astype(BF16), v) / den).astype(BF16))
        o_c = jnp.concatenate(heads, axis=-1)

        merged = None
        for br, (o, w_ref) in enumerate(((oa_ref[0, rows, :], w_a_ref), (ob_ref[0, rows, :], w_b_ref),
                                         (o_c, w_c_ref))):
            gcols = slice(br * D, (br + 1) * D)
            gate = jax.nn.sigmoid(_dot(h, w_gate_ref[:, gcols]) + b_gate_ref[:, gcols])
            term = gate * _dot(o, w_ref[...])
            merged = term if merged is None else merged + term
        mix = _dot(merged.astype(BF16), w_o_ref[...])
        out_ref[0, rows, :] = x + _rms_norm(mix, g_post_ref[...])


def _mix(x, o_a, o_b, q_c, kv_m, g_pre, g_post, w_gate, b_gate, w_a, w_b, w_c, w_o):
    B, S, D = x.shape
    tm = TOKEN_TILE
    tile = lambda w: pl.BlockSpec((1, tm, w), lambda b, t: (b, t, 0))
    consts = (g_pre, g_post, w_gate, b_gate, w_a, w_b, w_c, w_o)
    return pl.pallas_call(
        _mix_kernel,
        grid=(B, S // tm),
        in_specs=[tile(D), tile(SB_W), tile(DIL_W), tile(MEM_W),
                  pl.BlockSpec((1,) + kv_m.shape[1:], lambda b, t: (b, 0, 0))]
                 + [_const_spec(c.shape) for c in consts],
        out_specs=tile(D),
        out_shape=jax.ShapeDtypeStruct((B, S, D), F32),
        compiler_params=_params("parallel", "parallel"),
        name="mix",
    )(x, o_a, o_b, q_c, kv_m, *consts)


def _ffn_kernel(x_ref, g_pre_ref, g_post_ref, w_gate_ref, w_up_ref, w_out_ref, out_ref):
    for rows in _sub_tiles(x_ref.shape[0]):
        x = x_ref[rows, :]
        h = _rms_norm(x, g_pre_ref[...]).astype(BF16)
        f = jax.nn.silu(_dot(h, w_gate_ref[...])) * _dot(h, w_up_ref[...])
        f = _dot(f.astype(BF16), w_out_ref[...])
        out_ref[rows, :] = x + _rms_norm(f, g_post_ref[...])


def _ffn(x, g_pre, g_post, w_ff, w_out):
    N, D = x.shape
    d_ff = w_out.shape[0]
    tm = FFN_TILE
    half = lambda c: pl.BlockSpec((D, d_ff), lambda t: (0, c), pipeline_mode=pl.Buffered(1))
    return pl.pallas_call(
        _ffn_kernel,
        grid=(N // tm,),
        in_specs=[pl.BlockSpec((tm, D), lambda t: (t, 0)), _const_spec(g_pre.shape), _const_spec(g_post.shape),
                  half(0), half(1), _const_spec(w_out.shape)],
        out_specs=pl.BlockSpec((tm, D), lambda t: (t, 0)),
        out_shape=jax.ShapeDtypeStruct((N, D), F32),
        compiler_params=_params("parallel"),
        name="ffn",
    )(x, g_pre, g_post, w_ff, w_ff, w_out)


def _rope_tables(S):
    half = HEAD_DIM // 2
    inv_freq = ROPE_THETA ** (-jnp.arange(half, dtype=F32) * 2.0 / HEAD_DIM)
    ang = jnp.arange(S, dtype=F32)[:, None] * inv_freq[None, :]
    reps = LANES // half
    return jnp.tile(jnp.cos(ang), (1, reps)), jnp.tile(jnp.sin(ang), (1, reps))


def _rotary_layout(w):
    D = w.shape[0]
    w = w.reshape(D, DIL_HEADS, 2, HEAD_DIM // 2)
    return w.transpose(0, 2, 1, 3).reshape(D, DIL_W)


def _split_w_in(w):
    scale = HEAD_DIM ** -0.5
    w_sb = jnp.concatenate([w[:, :SB_W] * scale, w[:, SB_W:3 * SB_W]], axis=1).astype(BF16)
    w_dil = []
    off = 3 * SB_W
    for _ in DIL_GROUPS:
        q, k, v = (w[:, off + i * DIL_W:off + (i + 1) * DIL_W] for i in range(3))
        w_dil.append(jnp.concatenate([_rotary_layout(q) * scale, _rotary_layout(k), v], axis=1).astype(BF16))
        off += 3 * DIL_W
    w_qc = w[:, off:off + MEM_W].astype(BF16)
    return w_sb, w_qc, w_dil


def kernel(x, mem, g_pre_mix, g_post_mix, g_pre_ffn, g_post_ffn, g_mem, w_in, w_mem_kv, w_br_sb, w_br_dil,
           w_br_mem, w_gate, b_gate, w_o, w_ffn_in, w_ffn_out):
    B, S, D = x.shape
    depth = w_in.shape[0]
    d_ff = w_ffn_out.shape[1]
    cos, sin = _rope_tables(S)
    row = lambda v: v.reshape(1, -1)
    for l in range(depth):
        w_sb, w_qc, w_dil = _split_w_in(w_in[l])
        kv_m = _memkv(mem, row(g_mem[l]), w_mem_kv[l].astype(BF16))
        sb, q_c, *dils = _inproj(x, row(g_pre_mix[l]), cos, sin, w_sb, w_qc, w_dil)
        o_a = _stick(sb)
        o_b = _dilated(dils, S)
        x = _mix(x, o_a, o_b, q_c, kv_m, row(g_pre_mix[l]), row(g_post_mix[l]), w_gate[l].astype(BF16),
                 row(b_gate[l]), w_br_sb[l].astype(BF16), w_br_dil[l].astype(BF16), w_br_mem[l].astype(BF16),
                 w_o[l].astype(BF16))
        x = _ffn(x.reshape(B * S, D), row(g_pre_ffn[l]), row(g_post_ffn[l]), w_ffn_in[l].astype(BF16),
                 w_ffn_out[l].astype(BF16)).reshape(B, S, D)
    return x
```

```python
import functools

import jax
import jax.numpy as jnp
from jax import lax
from jax.experimental import pallas as pl
from jax.experimental.pallas import tpu as pltpu

HEAD_DIM = 64
SB_HEADS = 8
DIL_GROUPS = ((128, 1), (512, 4), (2048, 16))
DIL_HEADS = 4
MEM_HEADS = 4
MEM_HEAD_DIM = 128
N_BRANCHES = 3
BLOCK = 128
ROPE_THETA = 10000.0
NORM_EPS = 1e-6
NEG_INF = -1e30

SB_W = SB_HEADS * HEAD_DIM
DIL_W = DIL_HEADS * HEAD_DIM
MEM_W = MEM_HEADS * MEM_HEAD_DIM
N_GROUPS = len(DIL_GROUPS)

LOG2E = 1.4426950408889634
EXP2_UNDERFLOW = 127.0
QUERY_BLOCKS = 4
HEAD_BLOCKS = 2

LANES = 128
TOKEN_TILE = 1024
SUB_TILE = 512
FFN_TILE = 1024
MEMKV_TILE = 1024
VMEM_LIMIT_BYTES = 56 * 1024 * 1024

BF16 = jnp.bfloat16
F32 = jnp.float32


def _dot(a, b):
    return jnp.dot(a, b, preferred_element_type=F32)


def _dot_nt(a, b):
    return lax.dot_general(a, b, (((1,), (1,)), ((), ())), preferred_element_type=F32)


def _rms_norm(x, g):
    return x * lax.rsqrt(jnp.mean(x * x, axis=-1, keepdims=True) + NORM_EPS) * g


def _const_spec(shape):
    return pl.BlockSpec(shape, lambda *_: (0,) * len(shape), pipeline_mode=pl.Buffered(1))


def _params(*semantics):
    return pltpu.CompilerParams(dimension_semantics=semantics, vmem_limit_bytes=VMEM_LIMIT_BYTES)


def _sub_tiles(tm):
    return [slice(s, s + SUB_TILE) for s in range(0, tm, SUB_TILE)]


def _memkv_kernel(mem_ref, g_ref, w_ref, out_ref):
    h = _rms_norm(mem_ref[...], g_ref[...]).astype(BF16)
    out_ref[...] = _dot(h, w_ref[...]).astype(BF16)


def _memkv(mem, g_mem, w_kv):
    B, M, D = mem.shape
    rows = B * M
    tm = min(MEMKV_TILE, rows)
    return pl.pallas_call(
        _memkv_kernel,
        grid=(rows // tm,),
        in_specs=[pl.BlockSpec((tm, D), lambda t: (t, 0)),
                  _const_spec((1, D)),
                  _const_spec(w_kv.shape)],
        out_specs=pl.BlockSpec((tm, 2 * MEM_W), lambda t: (t, 0)),
        out_shape=jax.ShapeDtypeStruct((rows, 2 * MEM_W), BF16),
        compiler_params=_params("parallel"),
        name="memkv",
    )(mem.reshape(rows, D), g_mem, w_kv).reshape(B, M, 2 * MEM_W)


def _inproj_kernel(x_ref, g_ref, cos_ref, sin_ref, w_sb_ref, w_qc_ref, w_d0_ref, w_d1_ref, w_d2_ref,
                   sb_ref, qc_ref, d0_ref, d1_ref, d2_ref, split_ref):
    def rot(t, cos, sin):
        t1, t2 = t[:, :LANES], t[:, LANES:]
        return jnp.concatenate([t1 * cos - t2 * sin, t2 * cos + t1 * sin], axis=-1)

    for s, rows in enumerate(_sub_tiles(x_ref.shape[1])):
        h = _rms_norm(x_ref[0, rows, :], g_ref[...]).astype(BF16)
        sb = _dot(h, w_sb_ref[...])
        sb_ref[0, rows, :] = jnp.concatenate([sb[:, :SB_W] * LOG2E, sb[:, SB_W:]], axis=-1).astype(BF16)
        qc_ref[0, rows, :] = (_dot(h, w_qc_ref[...]) * (MEM_HEAD_DIM ** -0.5)).astype(BF16)
        cos = cos_ref[rows, :]
        sin = sin_ref[rows, :]
        for (_, dil), w_ref, out_ref in zip(DIL_GROUPS, (w_d0_ref, w_d1_ref, w_d2_ref),
                                            (d0_ref, d1_ref, d2_ref)):
            p = _dot(h, w_ref[...])
            p = jnp.concatenate([rot(p[:, :DIL_W], cos, sin) * LOG2E, rot(p[:, DIL_W:2 * DIL_W], cos, sin),
                                 p[:, 2 * DIL_W:]], axis=-1)
            if dil == 1:
                out_ref[0, 0, rows, :] = p.astype(BF16)
            else:
                nchunk = p.shape[1] // LANES
                per = SUB_TILE // dil
                for c in range(nchunk):
                    split_ref[s, c] = p[:, c * LANES:(c + 1) * LANES]
                for r in range(dil):
                    strided = pl.ds(r, per, stride=dil)
                    out_ref[0, r, s * per:(s + 1) * per, :] = jnp.concatenate(
                        [split_ref[s, c, strided, :] for c in range(nchunk)], axis=-1).astype(BF16)


def _inproj(x, g, cos, sin, w_sb, w_qc, w_dil):
    B, S, D = x.shape
    tm = TOKEN_TILE
    nt = S // tm
    tile = lambda w: pl.BlockSpec((1, tm, w), lambda b, t: (b, t, 0))
    dil_specs = [pl.BlockSpec((1, d, tm // d, 3 * DIL_W), lambda b, t: (b, 0, t, 0)) for _, d in DIL_GROUPS]
    dil_shapes = [jax.ShapeDtypeStruct((B, d, S // d, 3 * DIL_W), BF16) for _, d in DIL_GROUPS]
    return pl.pallas_call(
        _inproj_kernel,
        grid=(B, nt),
        in_specs=[tile(D), _const_spec((1, D)),
                  pl.BlockSpec((tm, LANES), lambda b, t: (t, 0)),
                  pl.BlockSpec((tm, LANES), lambda b, t: (t, 0)),
                  _const_spec(w_sb.shape), _const_spec(w_qc.shape)] + [_const_spec(w.shape) for w in w_dil],
        out_specs=[tile(3 * SB_W), tile(MEM_W)] + dil_specs,
        out_shape=[jax.ShapeDtypeStruct((B, S, 3 * SB_W), BF16),
                   jax.ShapeDtypeStruct((B, S, MEM_W), BF16)] + dil_shapes,
        scratch_shapes=[pltpu.VMEM((tm // SUB_TILE, 3 * DIL_W // LANES, SUB_TILE, LANES), F32)],
        compiler_params=_params("parallel", "parallel"),
        name="inproj",
    )(x, g, cos, sin, w_sb, w_qc, *w_dil)


def _stick_kernel(q_ref, k_ref, v_ref, o_ref, qq_ref, tri_ref, carry_ref, acc_ref):
    nblk = q_ref.shape[1] // BLOCK
    npair = q_ref.shape[2] // LANES
    pairs = range(npair)
    lane = lax.broadcasted_iota(jnp.int32, (BLOCK, LANES), 1)
    row = lax.broadcasted_iota(jnp.int32, (BLOCK, LANES), 0)
    first_head = lane < HEAD_DIM
    causal = jnp.concatenate([lane < row, lane < row], axis=0)
    zero = jnp.zeros((), BF16)
    tri = jnp.concatenate([(row >= lane), jnp.ones((BLOCK, LANES), jnp.bool_)], axis=1)
    tri = jnp.where(tri, 1.0, 0.0).astype(BF16)
    tri_ref[...] = jnp.concatenate([tri, tri], axis=0)

    def blocks(groups):
        cols = [slice(p * LANES, (p + 1) * LANES) for p in pairs]
        krows = lambda j: pl.ds(pl.multiple_of(j * BLOCK, BLOCK), BLOCK)
        zs = [[[_dot_nt(qq_ref[u, p], k_ref[0, krows(j), cols[p]]) for p in pairs] for j in js]
              for u, js, _ in groups]
        for g, (_, _, diagonal_first) in enumerate(groups):
            if diagonal_first:
                zs[g][0] = [jnp.where(causal, z, NEG_INF) for z in zs[g][0]]
        sums = [[[None] * npair for _ in js] for _, js, _ in groups]
        for g, (u, js, diagonal_first) in enumerate(groups):
            for e in range(len(js)):
                for p in pairs:
                    z = zs[g][e][p]
                    soft = jnp.maximum(z, 0.0) + jnp.log(1.0 + jnp.exp2(-jnp.abs(z))) * LOG2E
                    hi = soft.astype(BF16)
                    lo = (soft - hi.astype(F32)).astype(BF16)
                    sums[g][e][p] = _dot(jnp.concatenate([hi, lo], axis=1), tri_ref[...])
        leasts = []
        for g, (u, js, diagonal_first) in enumerate(groups):
            least = None
            for p in pairs:
                carry = None if diagonal_first else carry_ref[u, p]
                pws, v2s = [], []
                for e, j in enumerate(js):
                    after, total = sums[g][e][p][:, :LANES], sums[g][e][p][:, LANES:]
                    if carry is None:
                        carry = total
                    else:
                        after = after + carry
                        carry = carry + total
                    weight = jnp.exp2(zs[g][e][p] - after).astype(BF16)
                    pws.append(jnp.concatenate([weight[:BLOCK], weight[BLOCK:]], axis=1))
                    v = v_ref[0, krows(j), cols[p]]
                    v2s.append(jnp.concatenate([jnp.where(first_head, v, zero), jnp.where(first_head, zero, v)],
                                               axis=0))
                carry_ref[u, p] = carry
                least = carry if least is None else jnp.minimum(least, carry)
                out = _dot(jnp.concatenate(pws, axis=1), jnp.concatenate(v2s, axis=0))
                if diagonal_first:
                    acc_ref[u, p] = out
                else:
                    acc_ref[u, p] += out
            leasts.append(jnp.min(least))
        return leasts

    def q_blocks(first):
        slots = range(QUERY_BLOCKS)
        qrows = [pl.ds(pl.multiple_of((first + u) * BLOCK, BLOCK), BLOCK) for u in slots]
        for u in slots:
            for p in pairs:
                q = q_ref[0, qrows[u], p * LANES:(p + 1) * LANES]
                qq_ref[u, p] = jnp.concatenate([jnp.where(first_head, q, zero), jnp.where(first_head, zero, q)],
                                               axis=0)
        depth = [HEAD_BLOCKS if not isinstance(first, int) else min(HEAD_BLOCKS, first + u + 1) for u in slots]
        leasts = blocks([(u, [first + u - e for e in range(depth[u])], True) for u in slots])
        for u in slots:
            i = first + u

            def more(state, i=i):
                t, least_carry = state
                return jnp.logical_and(t < i, least_carry < EXP2_UNDERFLOW)

            def kv_step(state, i=i, u=u):
                t, _ = state
                return t + 1, blocks([(u, [i - 1 - t], False)])[0]

            lax.while_loop(more, kv_step, (jnp.int32(depth[u] - 1), leasts[u]))
            for p in pairs:
                o_ref[0, qrows[u], p * LANES:(p + 1) * LANES] = acc_ref[u, p].astype(o_ref.dtype)

    assert nblk % QUERY_BLOCKS == 0 and HEAD_BLOCKS <= QUERY_BLOCKS + 1
    q_blocks(0)

    def rest(t, _):
        q_blocks(t * QUERY_BLOCKS)
        return 0

    lax.fori_loop(1, nblk // QUERY_BLOCKS, rest, 0)


def _stick(sb):
    B, S, _ = sb.shape
    npair = SB_W // LANES
    spec = lambda c: pl.BlockSpec((1, S, SB_W), lambda b: (b, 0, c))
    return pl.pallas_call(
        _stick_kernel,
        grid=(B,),
        in_specs=[spec(0), spec(1), spec(2)],
        out_specs=spec(0),
        out_shape=jax.ShapeDtypeStruct((B, S, SB_W), BF16),
        scratch_shapes=[pltpu.VMEM((QUERY_BLOCKS, npair, 2 * BLOCK, LANES), BF16),
                        pltpu.VMEM((2 * BLOCK, 2 * LANES), BF16),
                        pltpu.VMEM((QUERY_BLOCKS, npair, 2 * BLOCK, LANES), F32),
                        pltpu.VMEM((QUERY_BLOCKS, npair, BLOCK, LANES), F32)],
        compiler_params=_params("parallel"),
        name="stick",
    )(sb, sb, sb)


def _dilated_kernel(d0_ref, d1_ref, d2_ref, o_ref, cap_ref, og_ref, lse_ref):
    S = o_ref.shape[1]
    nchunk = DIL_W // LANES
    span = BLOCK
    assert all(window // dil == span for window, dil in DIL_GROUPS)
    lane_qk = lax.broadcasted_iota(jnp.int32, (BLOCK, DIL_W), 1)
    qk_head = (lane_qk % LANES) // (HEAD_DIM // 2)
    first_head = lax.broadcasted_iota(jnp.int32, (BLOCK, LANES), 1) < HEAD_DIM
    zero = jnp.zeros((), BF16)

    qi = lax.broadcasted_iota(jnp.int32, (BLOCK, 2 * BLOCK), 0)
    kj = lax.broadcasted_iota(jnp.int32, (BLOCK, 2 * BLOCK), 1)
    for a in range(2):
        dist = qi + a * BLOCK - kj
        cap_ref[a] = jnp.where((dist >= 0) & (dist <= span), float(jnp.finfo(F32).max), NEG_INF)

    def key_window(ref, i):
        L = ref.shape[2]
        nk = min(2 * BLOCK, L)
        first = jnp.maximum(i - 1, 0) if nk < L else 0
        return first, nk

    def scores(ref, r, i):
        first, nk = key_window(ref, i)
        q = ref[0, r, pl.ds(i * BLOCK, BLOCK), 0:DIL_W]
        kk = ref[0, r, pl.ds(first * BLOCK, nk), DIL_W:2 * DIL_W]
        qs = jnp.concatenate([jnp.where(qk_head == h, q, zero) for h in range(DIL_HEADS)], axis=0)
        return _dot_nt(qs, kk)

    def attend(g, ref, dil, r, i, s):
        first, nk = key_window(ref, i)
        vv = ref[0, r, pl.ds(first * BLOCK, nk), 2 * DIL_W:3 * DIL_W]
        cap = cap_ref[i - first, :, 0:nk]
        s = jnp.minimum(s, jnp.concatenate([cap] * DIL_HEADS, axis=0))
        m = jnp.max(s, axis=-1, keepdims=True)
        p = jnp.exp2((s - m).astype(BF16))
        ones = jnp.ones((nk, LANES), BF16)
        rows = pl.ds(r + dil * BLOCK * i, BLOCK, stride=dil) if dil > 1 else pl.ds(i * BLOCK, BLOCK)
        per = LANES // HEAD_DIM
        for c in range(nchunk):
            top = slice(per * c * BLOCK, (per * c + 1) * BLOCK)
            bot = slice((per * c + 1) * BLOCK, (per * c + 2) * BLOCK)
            pv = _dot(p[per * c * BLOCK:per * (c + 1) * BLOCK],
                      jnp.concatenate([vv[:, c * LANES:(c + 1) * LANES], ones], axis=1))
            pv, den = pv[:, :LANES], pv[:, LANES:]
            den_c = jnp.where(first_head, den[:BLOCK], den[BLOCK:])
            m_c = jnp.where(first_head, m[top], m[bot])
            og_ref[g, c, rows, :] = jnp.where(first_head, pv[:BLOCK], pv[BLOCK:]) * (1.0 / den_c)
            lse_ref[g, c, rows, :] = m_c + jnp.log(den_c) * LOG2E

    side = 16
    for g, ((_, dil), ref) in enumerate(zip(DIL_GROUPS, (d0_ref, d1_ref, d2_ref))):
        nb = (S // dil) // BLOCK
        units = dil * nb
        assert units % side == 0

        def unit_group(t, _, g=g, ref=ref, dil=dil, nb=nb):
            where = [((side * t + u) // nb, (side * t + u) % nb) for u in range(side)]
            ss = [scores(ref, r, i) for r, i in where]
            for (r, i), s in zip(where, ss):
                attend(g, ref, dil, r, i, s)
            return 0

        lax.fori_loop(0, units // side, unit_group, 0)

    chunk = 2 * BLOCK

    def merge(t, _):
        rows = pl.ds(t * chunk, chunk)
        for c in range(nchunk):
            lses = [lse_ref[g, c, rows, :] for g in range(N_GROUPS)]
            top = functools.reduce(jnp.maximum, lses)
            ws = [jnp.exp2(l - top) for l in lses]
            num = functools.reduce(lambda a, b: a + b, [w * og_ref[g, c, rows, :] for g, w in enumerate(ws)])
            den = functools.reduce(lambda a, b: a + b, ws)
            o_ref[0, rows, c * LANES:(c + 1) * LANES] = (num / den).astype(o_ref.dtype)
        return 0

    lax.fori_loop(0, S // chunk, merge, 0)


def _dilated(dils, S):
    B = dils[0].shape[0]
    return pl.pallas_call(
        _dilated_kernel,
        grid=(B,),
        in_specs=[pl.BlockSpec((1,) + d.shape[1:], lambda b: (b, 0, 0, 0)) for d in dils],
        out_specs=pl.BlockSpec((1, S, DIL_W), lambda b: (b, 0, 0)),
        out_shape=jax.ShapeDtypeStruct((B, S, DIL_W), BF16),
        scratch_shapes=[pltpu.VMEM((2, BLOCK, 2 * BLOCK), F32)]
                       + [pltpu.VMEM((N_GROUPS, DIL_W // LANES, S, LANES), F32)] * 2,
        compiler_params=_params("parallel"),
        name="dilated",
    )(*dils)


def _mix_kernel(x_ref, oa_ref, ob_ref, qc_ref, kv_ref, g_pre_ref, g_post_ref, w_gate_ref, b_gate_ref,
                w_a_ref, w_b_ref, w_c_ref, w_o_ref, out_ref):
    D = x_ref.shape[-1]
    for rows in _sub_tiles(x_ref.shape[1]):
        x = x_ref[0, rows, :]
        h = _rms_norm(x, g_pre_ref[...]).astype(BF16)

        heads = []
        for hd in range(MEM_HEADS):
            cols = slice(hd * MEM_HEAD_DIM, (hd + 1) * MEM_HEAD_DIM)
            s = _dot_nt(qc_ref[0, rows, cols], kv_ref[0, :, cols])
            p = jnp.exp(s - jnp.max(s, axis=-1, keepdims=True))
            den = jnp.sum(p, axis=-1, keepdims=True)
            v = kv_ref[0, :, MEM_W + hd * MEM_HEAD_DIM:MEM_W + (hd + 1) * MEM_HEAD_DIM]
            heads.append((_dot(p.astype(BF16), v) / den).astype(BF16))
        o_c = jnp.concatenate(heads, axis=-1)

        merged = None
        for br, (o, w_ref) in enumerate(((oa_ref[0, rows, :], w_a_ref), (ob_ref[0, rows, :], w_b_ref),
                                         (o_c, w_c_ref))):
            gcols = slice(br * D, (br + 1) * D)
            gate = jax.nn.sigmoid(_dot(h, w_gate_ref[:, gcols]) + b_gate_ref[:, gcols])
            term = gate * _dot(o, w_ref[...])
            merged = term if merged is None else merged + term
        mix = _dot(merged.astype(BF16), w_o_ref[...])
        out_ref[0, rows, :] = x + _rms_norm(mix, g_post_ref[...])


def _mix(x, o_a, o_b, q_c, kv_m, g_pre, g_post, w_gate, b_gate, w_a, w_b, w_c, w_o):
    B, S, D = x.shape
    tm = TOKEN_TILE
    tile = lambda w: pl.BlockSpec((1, tm, w), lambda b, t: (b, t, 0))
    consts = (g_pre, g_post, w_gate, b_gate, w_a, w_b, w_c, w_o)
    return pl.pallas_call(
        _mix_kernel,
        grid=(B, S // tm),
        in_specs=[tile(D), tile(SB_W), tile(DIL_W), tile(MEM_W),
                  pl.BlockSpec((1,) + kv_m.shape[1:], lambda b, t: (b, 0, 0))]
                 + [_const_spec(c.shape) for c in consts],
        out_specs=tile(D),
        out_shape=jax.ShapeDtypeStruct((B, S, D), F32),
        compiler_params=_params("parallel", "parallel"),
        name="mix",
    )(x, o_a, o_b, q_c, kv_m, *consts)


def _ffn_kernel(x_ref, g_pre_ref, g_post_ref, w_gate_ref, w_up_ref, w_out_ref, out_ref):
    for rows in _sub_tiles(x_ref.shape[0]):
        x = x_ref[rows, :]
        h = _rms_norm(x, g_pre_ref[...]).astype(BF16)
        f = jax.nn.silu(_dot(h, w_gate_ref[...])) * _dot(h, w_up_ref[...])
        f = _dot(f.astype(BF16), w_out_ref[...])
        out_ref[rows, :] = x + _rms_norm(f, g_post_ref[...])


def _ffn(x, g_pre, g_post, w_ff, w_out):
    N, D = x.shape
    d_ff = w_out.shape[0]
    tm = FFN_TILE
    half = lambda c: pl.BlockSpec((D, d_ff), lambda t: (0, c), pipeline_mode=pl.Buffered(1))
    return pl.pallas_call(
        _ffn_kernel,
        grid=(N // tm,),
        in_specs=[pl.BlockSpec((tm, D), lambda t: (t, 0)), _const_spec(g_pre.shape), _const_spec(g_post.shape),
                  half(0), half(1), _const_spec(w_out.shape)],
        out_specs=pl.BlockSpec((tm, D), lambda t: (t, 0)),
        out_shape=jax.ShapeDtypeStruct((N, D), F32),
        compiler_params=_params("parallel"),
        name="ffn",
    )(x, g_pre, g_post, w_ff, w_ff, w_out)


def _rope_tables(S):
    half = HEAD_DIM // 2
    inv_freq = ROPE_THETA ** (-jnp.arange(half, dtype=F32) * 2.0 / HEAD_DIM)
    ang = jnp.arange(S, dtype=F32)[:, None] * inv_freq[None, :]
    reps = LANES // half
    return jnp.tile(jnp.cos(ang), (1, reps)), jnp.tile(jnp.sin(ang), (1, reps))


def _rotary_layout(w):
    D = w.shape[0]
    w = w.reshape(D, DIL_HEADS, 2, HEAD_DIM // 2)
    return w.transpose(0, 2, 1, 3).reshape(D, DIL_W)


def _split_w_in(w):
    scale = HEAD_DIM ** -0.5
    w_sb = jnp.concatenate([w[:, :SB_W] * scale, w[:, SB_W:3 * SB_W]], axis=1).astype(BF16)
    w_dil = []
    off = 3 * SB_W
    for _ in DIL_GROUPS:
        q, k, v = (w[:, off + i * DIL_W:off + (i + 1) * DIL_W] for i in range(3))
        w_dil.append(jnp.concatenate([_rotary_layout(q) * scale, _rotary_layout(k), v], axis=1).astype(BF16))
        off += 3 * DIL_W
    w_qc = w[:, off:off + MEM_W].astype(BF16)
    return w_sb, w_qc, w_dil


def kernel(x, mem, g_pre_mix, g_post_mix, g_pre_ffn, g_post_ffn, g_mem, w_in, w_mem_kv, w_br_sb, w_br_dil,
           w_br_mem, w_gate, b_gate, w_o, w_ffn_in, w_ffn_out):
    B, S, D = x.shape
    depth = w_in.shape[0]
    d_ff = w_ffn_out.shape[1]
    cos, sin = _rope_tables(S)
    row = lambda v: v.reshape(1, -1)
    for l in range(depth):
        w_sb, w_qc, w_dil = _split_w_in(w_in[l])
        kv_m = _memkv(mem, row(g_mem[l]), w_mem_kv[l].astype(BF16))
        sb, q_c, *dils = _inproj(x, row(g_pre_mix[l]), cos, sin, w_sb, w_qc, w_dil)
        o_a = _stick(sb)
        o_b = _dilated(dils, S)
        x = _mix(x, o_a, o_b, q_c, kv_m, row(g_pre_mix[l]), row(g_post_mix[l]), w_gate[l].astype(BF16),
                 row(b_gate[l]), w_br_sb[l].astype(BF16), w_br_dil[l].astype(BF16), w_br_mem[l].astype(BF16),
                 w_o[l].astype(BF16))
        x = _ffn(x.reshape(B * S, D), row(g_pre_ffn[l]), row(g_post_ffn[l]), w_ffn_in[l].astype(BF16),
                 w_ffn_out[l].astype(BF16)).reshape(B, S, D)
    return x
```

```python
import functools

import jax
import jax.numpy as jnp
from jax import lax
from jax.experimental import pallas as pl
from jax.experimental.pallas import tpu as pltpu

HEAD_DIM = 64
SB_HEADS = 8
DIL_GROUPS = ((128, 1), (512, 4), (2048, 16))
DIL_HEADS = 4
MEM_HEADS = 4
MEM_HEAD_DIM = 128
N_BRANCHES = 3
BLOCK = 128
ROPE_THETA = 10000.0
NORM_EPS = 1e-6
NEG_INF = -1e30

SB_W = SB_HEADS * HEAD_DIM
DIL_W = DIL_HEADS * HEAD_DIM
MEM_W = MEM_HEADS * MEM_HEAD_DIM
N_GROUPS = len(DIL_GROUPS)

LOG2E = 1.4426950408889634
EXP2_UNDERFLOW = 127.0
QUERY_BLOCKS = 4
HEAD_BLOCKS = 2

LANES = 128
TOKEN_TILE = 1024
SUB_TILE = 512
FFN_TILE = 1024
MEMKV_TILE = 1024
VMEM_LIMIT_BYTES = 56 * 1024 * 1024

BF16 = jnp.bfloat16
F32 = jnp.float32


def _dot(a, b):
    return jnp.dot(a, b, preferred_element_type=F32)


def _dot_nt(a, b):
    return lax.dot_general(a, b, (((1,), (1,)), ((), ())), preferred_element_type=F32)


def _rms_norm(x, g):
    return x * lax.rsqrt(jnp.mean(x * x, axis=-1, keepdims=True) + NORM_EPS) * g


def _const_spec(shape):
    return pl.BlockSpec(shape, lambda *_: (0,) * len(shape), pipeline_mode=pl.Buffered(1))


def _params(*semantics):
    return pltpu.CompilerParams(dimension_semantics=semantics, vmem_limit_bytes=VMEM_LIMIT_BYTES)


def _sub_tiles(tm):
    return [slice(s, s + SUB_TILE) for s in range(0, tm, SUB_TILE)]


def _memkv_kernel(mem_ref, g_ref, w_ref, out_ref):
    h = _rms_norm(mem_ref[...], g_ref[...]).astype(BF16)
    out_ref[...] = _dot(h, w_ref[...]).astype(BF16)


def _memkv(mem, g_mem, w_kv):
    B, M, D = mem.shape
    rows = B * M
    tm = min(MEMKV_TILE, rows)
    return pl.pallas_call(
        _memkv_kernel,
        grid=(rows // tm,),
        in_specs=[pl.BlockSpec((tm, D), lambda t: (t, 0)),
                  _const_spec((1, D)),
                  _const_spec(w_kv.shape)],
        out_specs=pl.BlockSpec((tm, 2 * MEM_W), lambda t: (t, 0)),
        out_shape=jax.ShapeDtypeStruct((rows, 2 * MEM_W), BF16),
        compiler_params=_params("parallel"),
        name="memkv",
    )(mem.reshape(rows, D), g_mem, w_kv).reshape(B, M, 2 * MEM_W)


def _inproj_kernel(x_ref, g_ref, cos_ref, sin_ref, w_sb_ref, w_qc_ref, w_d0_ref, w_d1_ref, w_d2_ref,
                   sb_ref, qc_ref, d0_ref, d1_ref, d2_ref, split_ref):
    def rot(t, cos, sin):
        t1, t2 = t[:, :LANES], t[:, LANES:]
        return jnp.concatenate([t1 * cos - t2 * sin, t2 * cos + t1 * sin], axis=-1)

    for s, rows in enumerate(_sub_tiles(x_ref.shape[1])):
        h = _rms_norm(x_ref[0, rows, :], g_ref[...]).astype(BF16)
        sb = _dot(h, w_sb_ref[...])
        sb_ref[0, rows, :] = jnp.concatenate([sb[:, :SB_W] * LOG2E, sb[:, SB_W:]], axis=-1).astype(BF16)
        qc_ref[0, rows, :] = (_dot(h, w_qc_ref[...]) * (MEM_HEAD_DIM ** -0.5)).astype(BF16)
        cos = cos_ref[rows, :]
        sin = sin_ref[rows, :]
        for (_, dil), w_ref, out_ref in zip(DIL_GROUPS, (w_d0_ref, w_d1_ref, w_d2_ref),
                                            (d0_ref, d1_ref, d2_ref)):
            p = _dot(h, w_ref[...])
            p = jnp.concatenate([rot(p[:, :DIL_W], cos, sin) * LOG2E, rot(p[:, DIL_W:2 * DIL_W], cos, sin),
                                 p[:, 2 * DIL_W:]], axis=-1)
            if dil == 1:
                out_ref[0, 0, rows, :] = p.astype(BF16)
            else:
                nchunk = p.shape[1] // LANES
                per = SUB_TILE // dil
                for c in range(nchunk):
                    split_ref[s, c] = p[:, c * LANES:(c + 1) * LANES]
                for r in range(dil):
                    strided = pl.ds(r, per, stride=dil)
                    out_ref[0, r, s * per:(s + 1) * per, :] = jnp.concatenate(
                        [split_ref[s, c, strided, :] for c in range(nchunk)], axis=-1).astype(BF16)


def _inproj(x, g, cos, sin, w_sb, w_qc, w_dil):
    B, S, D = x.shape
    tm = TOKEN_TILE
    nt = S // tm
    tile = lambda w: pl.BlockSpec((1, tm, w), lambda b, t: (b, t, 0))
    dil_specs = [pl.BlockSpec((1, d, tm // d, 3 * DIL_W), lambda b, t: (b, 0, t, 0)) for _, d in DIL_GROUPS]
    dil_shapes = [jax.ShapeDtypeStruct((B, d, S // d, 3 * DIL_W), BF16) for _, d in DIL_GROUPS]
    return pl.pallas_call(
        _inproj_kernel,
        grid=(B, nt),
        in_specs=[tile(D), _const_spec((1, D)),
                  pl.BlockSpec((tm, LANES), lambda b, t: (t, 0)),
                  pl.BlockSpec((tm, LANES), lambda b, t: (t, 0)),
                  _const_spec(w_sb.shape), _const_spec(w_qc.shape)] + [_const_spec(w.shape) for w in w_dil],
        out_specs=[tile(3 * SB_W), tile(MEM_W)] + dil_specs,
        out_shape=[jax.ShapeDtypeStruct((B, S, 3 * SB_W), BF16),
                   jax.ShapeDtypeStruct((B, S, MEM_W), BF16)] + dil_shapes,
        scratch_shapes=[pltpu.VMEM((tm // SUB_TILE, 3 * DIL_W // LANES, SUB_TILE, LANES), F32)],
        compiler_params=_params("parallel", "parallel"),
        name="inproj",
    )(x, g, cos, sin, w_sb, w_qc, *w_dil)


def _stick_kernel(q_ref, k_ref, v_ref, o_ref, qq_ref, tri_ref, carry_ref, acc_ref):
    nblk = q_ref.shape[1] // BLOCK
    npair = q_ref.shape[2] // LANES
    pairs = range(npair)
    lane = lax.broadcasted_iota(jnp.int32, (BLOCK, LANES), 1)
    row = lax.broadcasted_iota(jnp.int32, (BLOCK, LANES), 0)
    first_head = lane < HEAD_DIM
    causal = jnp.concatenate([lane < row, lane < row], axis=0)
    zero = jnp.zeros((), BF16)
    tri = jnp.concatenate([(row >= lane), jnp.ones((BLOCK, LANES), jnp.bool_)], axis=1)
    tri = jnp.where(tri, 1.0, 0.0).astype(BF16)
    tri_ref[...] = jnp.concatenate([tri, tri], axis=0)

    def blocks(groups):
        cols = [slice(p * LANES, (p + 1) * LANES) for p in pairs]
        krows = lambda j: pl.ds(pl.multiple_of(j * BLOCK, BLOCK), BLOCK)
        zs = [[[_dot_nt(qq_ref[u, p], k_ref[0, krows(j), cols[p]]) for p in pairs] for j in js]
              for u, js, _ in groups]
        for g, (_, _, diagonal_first) in enumerate(groups):
            if diagonal_first:
                zs[g][0] = [jnp.where(causal, z, NEG_INF) for z in zs[g][0]]
        sums = [[[None] * npair for _ in js] for _, js, _ in groups]
        for g, (u, js, diagonal_first) in enumerate(groups):
            for e in range(len(js)):
                for p in pairs:
                    z = zs[g][e][p]
                    soft = jnp.maximum(z, 0.0) + jnp.log(1.0 + jnp.exp2(-jnp.abs(z))) * LOG2E
                    hi = soft.astype(BF16)
                    lo = (soft - hi.astype(F32)).astype(BF16)
                    sums[g][e][p] = _dot(jnp.concatenate([hi, lo], axis=1), tri_ref[...])
        leasts = []
        for g, (u, js, diagonal_first) in enumerate(groups):
            least = None
            for p in pairs:
                carry = None if diagonal_first else carry_ref[u, p]
                pws, v2s = [], []
                for e, j in enumerate(js):
                    after, total = sums[g][e][p][:, :LANES], sums[g][e][p][:, LANES:]
                    if carry is None:
                        carry = total
                    else:
                        after = after + carry
                        carry = carry + total
                    weight = jnp.exp2(zs[g][e][p] - after).astype(BF16)
                    pws.append(jnp.concatenate([weight[:BLOCK], weight[BLOCK:]], axis=1))
                    v = v_ref[0, krows(j), cols[p]]
                    v2s.append(jnp.concatenate([jnp.where(first_head, v, zero), jnp.where(first_head, zero, v)],
                                               axis=0))
                carry_ref[u, p] = carry
                least = carry if least is None else jnp.minimum(least, carry)
                out = _dot(jnp.concatenate(pws, axis=1), jnp.concatenate(v2s, axis=0))
                if diagonal_first:
                    acc_ref[u, p] = out
                else:
                    acc_ref[u, p] += out
            leasts.append(jnp.min(least))
        return leasts

    def q_blocks(first):
        slots = range(QUERY_BLOCKS)
        qrows = [pl.ds(pl.multiple_of((first + u) * BLOCK, BLOCK), BLOCK) for u in slots]
        for u in slots:
            for p in pairs:
                q = q_ref[0, qrows[u], p * LANES:(p + 1) * LANES]
                qq_ref[u, p] = jnp.concatenate([jnp.where(first_head, q, zero), jnp.where(first_head, zero, q)],
                                               axis=0)
        depth = [HEAD_BLOCKS if not isinstance(first, int) else min(HEAD_BLOCKS, first + u + 1) for u in slots]
        leasts = blocks([(u, [first + u - e for e in range(depth[u])], True) for u in slots])
        for u in slots:
            i = first + u

            def more(state, i=i):
                t, least_carry = state
                return jnp.logical_and(t < i, least_carry < EXP2_UNDERFLOW)

            def kv_step(state, i=i, u=u):
                t, _ = state
                return t + 1, blocks([(u, [i - 1 - t], False)])[0]

            lax.while_loop(more, kv_step, (jnp.int32(depth[u] - 1), leasts[u]))
            for p in pairs:
                o_ref[0, qrows[u], p * LANES:(p + 1) * LANES] = acc_ref[u, p].astype(o_ref.dtype)

    assert nblk % QUERY_BLOCKS == 0 and HEAD_BLOCKS <= QUERY_BLOCKS + 1
    q_blocks(0)

    def rest(t, _):
        q_blocks(t * QUERY_BLOCKS)
        return 0

    lax.fori_loop(1, nblk // QUERY_BLOCKS, rest, 0)


def _stick(sb):
    B, S, _ = sb.shape
    npair = SB_W // LANES
    spec = lambda c: pl.BlockSpec((1, S, SB_W), lambda b: (b, 0, c))
    return pl.pallas_call(
        _stick_kernel,
        grid=(B,),
        in_specs=[spec(0), spec(1), spec(2)],
        out_specs=spec(0),
        out_shape=jax.ShapeDtypeStruct((B, S, SB_W), BF16),
        scratch_shapes=[pltpu.VMEM((QUERY_BLOCKS, npair, 2 * BLOCK, LANES), BF16),
                        pltpu.VMEM((2 * BLOCK, 2 * LANES), BF16),
                        pltpu.VMEM((QUERY_BLOCKS, npair, 2 * BLOCK, LANES), F32),
                        pltpu.VMEM((QUERY_BLOCKS, npair, BLOCK, LANES), F32)],
        compiler_params=_params("parallel"),
        name="stick",
    )(sb, sb, sb)


def _dilated_kernel(d0_ref, d1_ref, d2_ref, o_ref, cap_ref, og_ref, lse_ref):
    S = o_ref.shape[1]
    nchunk = DIL_W // LANES
    span = BLOCK
    assert all(window // dil == span for window, dil in DIL_GROUPS)
    lane_qk = lax.broadcasted_iota(jnp.int32, (BLOCK, DIL_W), 1)
    qk_head = (lane_qk % LANES) // (HEAD_DIM // 2)
    first_head = lax.broadcasted_iota(jnp.int32, (BLOCK, LANES), 1) < HEAD_DIM
    zero = jnp.zeros((), BF16)

    qi = lax.broadcasted_iota(jnp.int32, (BLOCK, 2 * BLOCK), 0)
    kj = lax.broadcasted_iota(jnp.int32, (BLOCK, 2 * BLOCK), 1)
    for a in range(2):
        dist = qi + a * BLOCK - kj
        cap_ref[a] = jnp.where((dist >= 0) & (dist <= span), float(jnp.finfo(F32).max), NEG_INF)

    def key_window(ref, i):
        L = ref.shape[2]
        nk = min(2 * BLOCK, L)
        first = jnp.maximum(i - 1, 0) if nk < L else 0
        return first, nk

    def scores(ref, r, i):
        first, nk = key_window(ref, i)
        q = ref[0, r, pl.ds(i * BLOCK, BLOCK), 0:DIL_W]
        kk = ref[0, r, pl.ds(first * BLOCK, nk), DIL_W:2 * DIL_W]
        qs = jnp.concatenate([jnp.where(qk_head == h, q, zero) for h in range(DIL_HEADS)], axis=0)
        return _dot_nt(qs, kk)

    def attend(g, ref, dil, r, i, s):
        first, nk = key_window(ref, i)
        vv = ref[0, r, pl.ds(first * BLOCK, nk), 2 * DIL_W:3 * DIL_W]
        cap = cap_ref[i - first, :, 0:nk]
        s = jnp.minimum(s, jnp.concatenate([cap] * DIL_HEADS, axis=0))
        m = jnp.max(s, axis=-1, keepdims=True)
        p = jnp.exp2((s - m).astype(BF16))
        ones = jnp.ones((nk, LANES), BF16)
        rows = pl.ds(r + dil * BLOCK * i, BLOCK, stride=dil) if dil > 1 else pl.ds(i * BLOCK, BLOCK)
        per = LANES // HEAD_DIM
        for c in range(nchunk):
            top = slice(per * c * BLOCK, (per * c + 1) * BLOCK)
            bot = slice((per * c + 1) * BLOCK, (per * c + 2) * BLOCK)
            pv = _dot(p[per * c * BLOCK:per * (c + 1) * BLOCK],
                      jnp.concatenate([vv[:, c * LANES:(c + 1) * LANES], ones], axis=1))
            pv, den = pv[:, :LANES], pv[:, LANES:]
            den_c = jnp.where(first_head, den[:BLOCK], den[BLOCK:])
            m_c = jnp.where(first_head, m[top], m[bot])
            og_ref[g, c, rows, :] = jnp.where(first_head, pv[:BLOCK], pv[BLOCK:]) * (1.0 / den_c)
            lse_ref[g, c, rows, :] = m_c + jnp.log(den_c) * LOG2E

    side = 16
    ahead = 1
    for g, ((_, dil), ref) in enumerate(zip(DIL_GROUPS, (d0_ref, d1_ref, d2_ref))):
        nb = (S // dil) // BLOCK
        units = dil * nb
        assert units % side == 0 and side % ahead == 0

        def unit_group(t, _, g=g, ref=ref, dil=dil, nb=nb):
            where = [((side * t + u) // nb, (side * t + u) % nb) for u in range(side)]
            batches = [where[k:k + ahead] for k in range(0, side, ahead)]
            coming = [scores(ref, r, i) for r, i in batches[0]]
            for k, batch in enumerate(batches):
                ss = coming
                if k + 1 < len(batches):
                    coming = [scores(ref, r, i) for r, i in batches[k + 1]]
                for (r, i), s in zip(batch, ss):
                    attend(g, ref, dil, r, i, s)
            return 0

        lax.fori_loop(0, units // side, unit_group, 0)

    chunk = 2 * BLOCK

    def merge(t, _):
        rows = pl.ds(t * chunk, chunk)
        for c in range(nchunk):
            lses = [lse_ref[g, c, rows, :] for g in range(N_GROUPS)]
            top = functools.reduce(jnp.maximum, lses)
            ws = [jnp.exp2(l - top) for l in lses]
            num = functools.reduce(lambda a, b: a + b, [w * og_ref[g, c, rows, :] for g, w in enumerate(ws)])
            den = functools.reduce(lambda a, b: a + b, ws)
            o_ref[0, rows, c * LANES:(c + 1) * LANES] = (num / den).astype(o_ref.dtype)
        return 0

    lax.fori_loop(0, S // chunk, merge, 0)


def _dilated(dils, S):
    B = dils[0].shape[0]
    return pl.pallas_call(
        _dilated_kernel,
        grid=(B,),
        in_specs=[pl.BlockSpec((1,) + d.shape[1:], lambda b: (b, 0, 0, 0)) for d in dils],
        out_specs=pl.BlockSpec((1, S, DIL_W), lambda b: (b, 0, 0)),
        out_shape=jax.ShapeDtypeStruct((B, S, DIL_W), BF16),
        scratch_shapes=[pltpu.VMEM((2, BLOCK, 2 * BLOCK), F32)]
                       + [pltpu.VMEM((N_GROUPS, DIL_W // LANES, S, LANES), F32)] * 2,
        compiler_params=_params("parallel"),
        name="dilated",
    )(*dils)


def _mix_kernel(x_ref, oa_ref, ob_ref, qc_ref, kv_ref, g_pre_ref, g_post_ref, w_gate_ref, b_gate_ref,
                w_a_ref, w_b_ref, w_c_ref, w_o_ref, out_ref):
    D = x_ref.shape[-1]
    for rows in _sub_tiles(x_ref.shape[1]):
        x = x_ref[0, rows, :]
        h = _rms_norm(x, g_pre_ref[...]).astype(BF16)

        heads = []
        for hd in range(MEM_HEADS):
            cols = slice(hd * MEM_HEAD_DIM, (hd + 1) * MEM_HEAD_DIM)
            s = _dot_nt(qc_ref[0, rows, cols], kv_ref[0, :, cols])
            p = jnp.exp(s - jnp.max(s, axis=-1, keepdims=True))
            den = jnp.sum(p, axis=-1, keepdims=True)
            v = kv_ref[0, :, MEM_W + hd * MEM_HEAD_DIM:MEM_W + (hd + 1) * MEM_HEAD_DIM]
            heads.append((_dot(p.astype(BF16), v) / den).astype(BF16))
        o_c = jnp.concatenate(heads, axis=-1)

        merged = None
        for br, (o, w_ref) in enumerate(((oa_ref[0, rows, :], w_a_ref), (ob_ref[0, rows, :], w_b_ref),
                                         (o_c, w_c_ref))):
            gcols = slice(br * D, (br + 1) * D)
            gate = jax.nn.sigmoid(_dot(h, w_gate_ref[:, gcols]) + b_gate_ref[:, gcols])
            term = gate * _dot(o, w_ref[...])
            merged = term if merged is None else merged + term
        mix = _dot(merged.astype(BF16), w_o_ref[...])
        out_ref[0, rows, :] = x + _rms_norm(mix, g_post_ref[...])


def _mix(x, o_a, o_b, q_c, kv_m, g_pre, g_post, w_gate, b_gate, w_a, w_b, w_c, w_o):
    B, S, D = x.shape
    tm = TOKEN_TILE
    tile = lambda w: pl.BlockSpec((1, tm, w), lambda b, t: (b, t, 0))
    consts = (g_pre, g_post, w_gate, b_gate, w_a, w_b, w_c, w_o)
    return pl.pallas_call(
        _mix_kernel,
        grid=(B, S // tm),
        in_specs=[tile(D), tile(SB_W), tile(DIL_W), tile(MEM_W),
                  pl.BlockSpec((1,) + kv_m.shape[1:], lambda b, t: (b, 0, 0))]
                 + [_const_spec(c.shape) for c in consts],
        out_specs=tile(D),
        out_shape=jax.ShapeDtypeStruct((B, S, D), F32),
        compiler_params=_params("parallel", "parallel"),
        name="mix",
    )(x, o_a, o_b, q_c, kv_m, *consts)


def _ffn_kernel(x_ref, g_pre_ref, g_post_ref, w_gate_ref, w_up_ref, w_out_ref, out_ref):
    for rows in _sub_tiles(x_ref.shape[0]):
        x = x_ref[rows, :]
        h = _rms_norm(x, g_pre_ref[...]).astype(BF16)
        f = jax.nn.silu(_dot(h, w_gate_ref[...])) * _dot(h, w_up_ref[...])
        f = _dot(f.astype(BF16), w_out_ref[...])
        out_ref[rows, :] = x + _rms_norm(f, g_post_ref[...])


def _ffn(x, g_pre, g_post, w_ff, w_out):
    N, D = x.shape
    d_ff = w_out.shape[0]
    tm = FFN_TILE
    half = lambda c: pl.BlockSpec((D, d_ff), lambda t: (0, c), pipeline_mode=pl.Buffered(1))
    return pl.pallas_call(
        _ffn_kernel,
        grid=(N // tm,),
        in_specs=[pl.BlockSpec((tm, D), lambda t: (t, 0)), _const_spec(g_pre.shape), _const_spec(g_post.shape),
                  half(0), half(1), _const_spec(w_out.shape)],
        out_specs=pl.BlockSpec((tm, D), lambda t: (t, 0)),
        out_shape=jax.ShapeDtypeStruct((N, D), F32),
        compiler_params=_params("parallel"),
        name="ffn",
    )(x, g_pre, g_post, w_ff, w_ff, w_out)


def _rope_tables(S):
    half = HEAD_DIM // 2
    inv_freq = ROPE_THETA ** (-jnp.arange(half, dtype=F32) * 2.0 / HEAD_DIM)
    ang = jnp.arange(S, dtype=F32)[:, None] * inv_freq[None, :]
    reps = LANES // half
    return jnp.tile(jnp.cos(ang), (1, reps)), jnp.tile(jnp.sin(ang), (1, reps))


def _rotary_layout(w):
    D = w.shape[0]
    w = w.reshape(D, DIL_HEADS, 2, HEAD_DIM // 2)
    return w.transpose(0, 2, 1, 3).reshape(D, DIL_W)


def _split_w_in(w):
    scale = HEAD_DIM ** -0.5
    w_sb = jnp.concatenate([w[:, :SB_W] * scale, w[:, SB_W:3 * SB_W]], axis=1).astype(BF16)
    w_dil = []
    off = 3 * SB_W
    for _ in DIL_GROUPS:
        q, k, v = (w[:, off + i * DIL_W:off + (i + 1) * DIL_W] for i in range(3))
        w_dil.append(jnp.concatenate([_rotary_layout(q) * scale, _rotary_layout(k), v], axis=1).astype(BF16))
        off += 3 * DIL_W
    w_qc = w[:, off:off + MEM_W].astype(BF16)
    return w_sb, w_qc, w_dil


def kernel(x, mem, g_pre_mix, g_post_mix, g_pre_ffn, g_post_ffn, g_mem, w_in, w_mem_kv, w_br_sb, w_br_dil,
           w_br_mem, w_gate, b_gate, w_o, w_ffn_in, w_ffn_out):
    B, S, D = x.shape
    depth = w_in.shape[0]
    d_ff = w_ffn_out.shape[1]
    cos, sin = _rope_tables(S)
    row = lambda v: v.reshape(1, -1)
    for l in range(depth):
        w_sb, w_qc, w_dil = _split_w_in(w_in[l])
        kv_m = _memkv(mem, row(g_mem[l]), w_mem_kv[l].astype(BF16))
        sb, q_c, *dils = _inproj(x, row(g_pre_mix[l]), cos, sin, w_sb, w_qc, w_dil)
        o_a = _stick(sb)
        o_b = _dilated(dils, S)
        x = _mix(x, o_a, o_b, q_c, kv_m, row(g_pre_mix[l]), row(g_post_mix[l]), w_gate[l].astype(BF16),
                 row(b_gate[l]), w_br_sb[l].astype(BF16), w_br_dil[l].astype(BF16), w_br_mem[l].astype(BF16),
                 w_o[l].astype(BF16))
        x = _ffn(x.reshape(B * S, D), row(g_pre_ffn[l]), row(g_post_ffn[l]), w_ffn_in[l].astype(BF16),
                 w_ffn_out[l].astype(BF16)).reshape(B, S, D)
    return x
```

```python
import functools

import jax
import jax.numpy as jnp
from jax import lax
from jax.experimental import pallas as pl
from jax.experimental.pallas import tpu as pltpu

HEAD_DIM = 64
SB_HEADS = 8
DIL_GROUPS = ((128, 1), (512, 4), (2048, 16))
DIL_HEADS = 4
MEM_HEADS = 4
MEM_HEAD_DIM = 128
N_BRANCHES = 3
BLOCK = 128
ROPE_THETA = 10000.0
NORM_EPS = 1e-6
NEG_INF = -1e30

SB_W = SB_HEADS * HEAD_DIM
DIL_W = DIL_HEADS * HEAD_DIM
MEM_W = MEM_HEADS * MEM_HEAD_DIM
N_GROUPS = len(DIL_GROUPS)

LOG2E = 1.4426950408889634
EXP2_UNDERFLOW = 127.0
QUERY_BLOCKS = 4
HEAD_BLOCKS = 2

LANES = 128
TOKEN_TILE = 1024
SUB_TILE = 512
FFN_TILE = 1024
FFN_CHUNKS = 11
MEMKV_TILE = 1024
VMEM_LIMIT_BYTES = 56 * 1024 * 1024

BF16 = jnp.bfloat16
F32 = jnp.float32


def _dot(a, b):
    return jnp.dot(a, b, preferred_element_type=F32)


def _dot_nt(a, b):
    return lax.dot_general(a, b, (((1,), (1,)), ((), ())), preferred_element_type=F32)


def _rms_norm(x, g):
    return x * lax.rsqrt(jnp.mean(x * x, axis=-1, keepdims=True) + NORM_EPS) * g


def _const_spec(shape):
    return pl.BlockSpec(shape, lambda *_: (0,) * len(shape), pipeline_mode=pl.Buffered(1))


def _params(*semantics):
    return pltpu.CompilerParams(dimension_semantics=semantics, vmem_limit_bytes=VMEM_LIMIT_BYTES)


def _sub_tiles(tm):
    return [slice(s, s + SUB_TILE) for s in range(0, tm, SUB_TILE)]


def _memkv_kernel(mem_ref, g_ref, w_ref, out_ref):
    h = _rms_norm(mem_ref[...], g_ref[...]).astype(BF16)
    out_ref[...] = _dot(h, w_ref[...]).astype(BF16)


def _memkv(mem, g_mem, w_kv):
    B, M, D = mem.shape
    rows = B * M
    tm = min(MEMKV_TILE, rows)
    return pl.pallas_call(
        _memkv_kernel,
        grid=(rows // tm,),
        in_specs=[pl.BlockSpec((tm, D), lambda t: (t, 0)),
                  _const_spec((1, D)),
                  _const_spec(w_kv.shape)],
        out_specs=pl.BlockSpec((tm, 2 * MEM_W), lambda t: (t, 0)),
        out_shape=jax.ShapeDtypeStruct((rows, 2 * MEM_W), BF16),
        compiler_params=_params("parallel"),
        name="memkv",
    )(mem.reshape(rows, D), g_mem, w_kv).reshape(B, M, 2 * MEM_W)


def _inproj_kernel(x_ref, g_ref, cos_ref, sin_ref, w_sb_ref, w_qc_ref, w_d0_ref, w_d1_ref, w_d2_ref,
                   sb_ref, qc_ref, d0_ref, d1_ref, d2_ref, split_ref):
    def rot(t, cos, sin):
        t1, t2 = t[:, :LANES], t[:, LANES:]
        return jnp.concatenate([t1 * cos - t2 * sin, t2 * cos + t1 * sin], axis=-1)

    for s, rows in enumerate(_sub_tiles(x_ref.shape[1])):
        h = _rms_norm(x_ref[0, rows, :], g_ref[...]).astype(BF16)
        sb = _dot(h, w_sb_ref[...])
        sb_ref[0, rows, :] = jnp.concatenate([sb[:, :SB_W] * LOG2E, sb[:, SB_W:]], axis=-1).astype(BF16)
        qc_ref[0, rows, :] = (_dot(h, w_qc_ref[...]) * (MEM_HEAD_DIM ** -0.5)).astype(BF16)
        cos = cos_ref[rows, :]
        sin = sin_ref[rows, :]
        for (_, dil), w_ref, out_ref in zip(DIL_GROUPS, (w_d0_ref, w_d1_ref, w_d2_ref),
                                            (d0_ref, d1_ref, d2_ref)):
            p = _dot(h, w_ref[...])
            p = jnp.concatenate([rot(p[:, :DIL_W], cos, sin) * LOG2E, rot(p[:, DIL_W:2 * DIL_W], cos, sin),
                                 p[:, 2 * DIL_W:]], axis=-1)
            if dil == 1:
                out_ref[0, 0, rows, :] = p.astype(BF16)
            else:
                nchunk = p.shape[1] // LANES
                per = SUB_TILE // dil
                for c in range(nchunk):
                    split_ref[s, c] = p[:, c * LANES:(c + 1) * LANES]
                for r in range(dil):
                    strided = pl.ds(r, per, stride=dil)
                    out_ref[0, r, s * per:(s + 1) * per, :] = jnp.concatenate(
                        [split_ref[s, c, strided, :] for c in range(nchunk)], axis=-1).astype(BF16)


def _inproj(x, g, cos, sin, w_sb, w_qc, w_dil):
    B, S, D = x.shape
    tm = TOKEN_TILE
    nt = S // tm
    tile = lambda w: pl.BlockSpec((1, tm, w), lambda b, t: (b, t, 0))
    dil_specs = [pl.BlockSpec((1, d, tm // d, 3 * DIL_W), lambda b, t: (b, 0, t, 0)) for _, d in DIL_GROUPS]
    dil_shapes = [jax.ShapeDtypeStruct((B, d, S // d, 3 * DIL_W), BF16) for _, d in DIL_GROUPS]
    return pl.pallas_call(
        _inproj_kernel,
        grid=(B, nt),
        in_specs=[tile(D), _const_spec((1, D)),
                  pl.BlockSpec((tm, LANES), lambda b, t: (t, 0)),
                  pl.BlockSpec((tm, LANES), lambda b, t: (t, 0)),
                  _const_spec(w_sb.shape), _const_spec(w_qc.shape)] + [_const_spec(w.shape) for w in w_dil],
        out_specs=[tile(3 * SB_W), tile(MEM_W)] + dil_specs,
        out_shape=[jax.ShapeDtypeStruct((B, S, 3 * SB_W), BF16),
                   jax.ShapeDtypeStruct((B, S, MEM_W), BF16)] + dil_shapes,
        scratch_shapes=[pltpu.VMEM((tm // SUB_TILE, 3 * DIL_W // LANES, SUB_TILE, LANES), F32)],
        compiler_params=_params("parallel", "parallel"),
        name="inproj",
    )(x, g, cos, sin, w_sb, w_qc, *w_dil)


def _stick_kernel(q_ref, k_ref, v_ref, o_ref, qq_ref, tri_ref, carry_ref, acc_ref):
    nblk = q_ref.shape[1] // BLOCK
    npair = q_ref.shape[2] // LANES
    pairs = range(npair)
    lane = lax.broadcasted_iota(jnp.int32, (BLOCK, LANES), 1)
    row = lax.broadcasted_iota(jnp.int32, (BLOCK, LANES), 0)
    first_head = lane < HEAD_DIM
    causal = jnp.concatenate([lane < row, lane < row], axis=0)
    zero = jnp.zeros((), BF16)
    tri = jnp.concatenate([(row >= lane), jnp.ones((BLOCK, LANES), jnp.bool_)], axis=1)
    tri = jnp.where(tri, 1.0, 0.0).astype(BF16)
    tri_ref[...] = jnp.concatenate([tri, tri], axis=0)

    def blocks(groups):
        cols = [slice(p * LANES, (p + 1) * LANES) for p in pairs]
        krows = lambda j: pl.ds(pl.multiple_of(j * BLOCK, BLOCK), BLOCK)
        zs = [[[_dot_nt(qq_ref[u, p], k_ref[0, krows(j), cols[p]]) for p in pairs] for j in js]
              for u, js, _ in groups]
        for g, (_, _, diagonal_first) in enumerate(groups):
            if diagonal_first:
                zs[g][0] = [jnp.where(causal, z, NEG_INF) for z in zs[g][0]]
        sums = [[[None] * npair for _ in js] for _, js, _ in groups]
        for g, (u, js, diagonal_first) in enumerate(groups):
            for e in range(len(js)):
                for p in pairs:
                    z = zs[g][e][p]
                    soft = jnp.maximum(z, 0.0) + jnp.log(1.0 + jnp.exp2(-jnp.abs(z))) * LOG2E
                    hi = soft.astype(BF16)
                    lo = (soft - hi.astype(F32)).astype(BF16)
                    sums[g][e][p] = _dot(jnp.concatenate([hi, lo], axis=1), tri_ref[...])
        leasts = []
        for g, (u, js, diagonal_first) in enumerate(groups):
            least = None
            for p in pairs:
                carry = None if diagonal_first else carry_ref[u, p]
                pws, v2s = [], []
                for e, j in enumerate(js):
                    after, total = sums[g][e][p][:, :LANES], sums[g][e][p][:, LANES:]
                    if carry is None:
                        carry = total
                    else:
                        after = after + carry
                        carry = carry + total
                    weight = jnp.exp2(zs[g][e][p] - after).astype(BF16)
                    pws.append(jnp.concatenate([weight[:BLOCK], weight[BLOCK:]], axis=1))
                    v = v_ref[0, krows(j), cols[p]]
                    v2s.append(jnp.concatenate([jnp.where(first_head, v, zero), jnp.where(first_head, zero, v)],
                                               axis=0))
                carry_ref[u, p] = carry
                least = carry if least is None else jnp.minimum(least, carry)
                out = _dot(jnp.concatenate(pws, axis=1), jnp.concatenate(v2s, axis=0))
                if diagonal_first:
                    acc_ref[u, p] = out
                else:
                    acc_ref[u, p] += out
            leasts.append(jnp.min(least))
        return leasts

    def q_blocks(first):
        slots = range(QUERY_BLOCKS)
        qrows = [pl.ds(pl.multiple_of((first + u) * BLOCK, BLOCK), BLOCK) for u in slots]
        for u in slots:
            for p in pairs:
                q = q_ref[0, qrows[u], p * LANES:(p + 1) * LANES]
                qq_ref[u, p] = jnp.concatenate([jnp.where(first_head, q, zero), jnp.where(first_head, zero, q)],
                                               axis=0)
        depth = [HEAD_BLOCKS if not isinstance(first, int) else min(HEAD_BLOCKS, first + u + 1) for u in slots]
        leasts = blocks([(u, [first + u - e for e in range(depth[u])], True) for u in slots])
        for u in slots:
            i = first + u

            def more(state, i=i):
                t, least_carry = state
                return jnp.logical_and(t < i, least_carry < EXP2_UNDERFLOW)

            def kv_step(state, i=i, u=u):
                t, _ = state
                return t + 1, blocks([(u, [i - 1 - t], False)])[0]

            lax.while_loop(more, kv_step, (jnp.int32(depth[u] - 1), leasts[u]))
            for p in pairs:
                o_ref[0, qrows[u], p * LANES:(p + 1) * LANES] = acc_ref[u, p].astype(o_ref.dtype)

    assert nblk % QUERY_BLOCKS == 0 and HEAD_BLOCKS <= QUERY_BLOCKS + 1
    q_blocks(0)

    def rest(t, _):
        q_blocks(t * QUERY_BLOCKS)
        return 0

    lax.fori_loop(1, nblk // QUERY_BLOCKS, rest, 0)


def _stick(sb):
    B, S, _ = sb.shape
    npair = SB_W // LANES
    spec = lambda c: pl.BlockSpec((1, S, SB_W), lambda b: (b, 0, c))
    return pl.pallas_call(
        _stick_kernel,
        grid=(B,),
        in_specs=[spec(0), spec(1), spec(2)],
        out_specs=spec(0),
        out_shape=jax.ShapeDtypeStruct((B, S, SB_W), BF16),
        scratch_shapes=[pltpu.VMEM((QUERY_BLOCKS, npair, 2 * BLOCK, LANES), BF16),
                        pltpu.VMEM((2 * BLOCK, 2 * LANES), BF16),
                        pltpu.VMEM((QUERY_BLOCKS, npair, 2 * BLOCK, LANES), F32),
                        pltpu.VMEM((QUERY_BLOCKS, npair, BLOCK, LANES), F32)],
        compiler_params=_params("parallel"),
        name="stick",
    )(sb, sb, sb)


def _dilated_kernel(d0_ref, d1_ref, d2_ref, o_ref, cap_ref, og_ref, lse_ref):
    S = o_ref.shape[1]
    nchunk = DIL_W // LANES
    span = BLOCK
    assert all(window // dil == span for window, dil in DIL_GROUPS)
    lane_qk = lax.broadcasted_iota(jnp.int32, (BLOCK, DIL_W), 1)
    qk_head = (lane_qk % LANES) // (HEAD_DIM // 2)
    first_head = lax.broadcasted_iota(jnp.int32, (BLOCK, LANES), 1) < HEAD_DIM
    zero = jnp.zeros((), BF16)

    qi = lax.broadcasted_iota(jnp.int32, (BLOCK, 2 * BLOCK), 0)
    kj = lax.broadcasted_iota(jnp.int32, (BLOCK, 2 * BLOCK), 1)
    for a in range(2):
        dist = qi + a * BLOCK - kj
        cap_ref[a] = jnp.where((dist >= 0) & (dist <= span), float(jnp.finfo(F32).max), NEG_INF)

    def key_window(ref, i):
        L = ref.shape[2]
        nk = min(2 * BLOCK, L)
        first = jnp.maximum(i - 1, 0) if nk < L else 0
        return first, nk

    def scores(ref, r, i):
        first, nk = key_window(ref, i)
        q = ref[0, r, pl.ds(i * BLOCK, BLOCK), 0:DIL_W]
        kk = ref[0, r, pl.ds(first * BLOCK, nk), DIL_W:2 * DIL_W]
        qs = jnp.concatenate([jnp.where(qk_head == h, q, zero) for h in range(DIL_HEADS)], axis=0)
        return _dot_nt(qs, kk)

    def attend(g, ref, dil, r, i, s):
        first, nk = key_window(ref, i)
        vv = ref[0, r, pl.ds(first * BLOCK, nk), 2 * DIL_W:3 * DIL_W]
        cap = cap_ref[i - first, :, 0:nk]
        s = jnp.minimum(s, jnp.concatenate([cap] * DIL_HEADS, axis=0))
        m = jnp.max(s, axis=-1, keepdims=True)
        p = jnp.exp2((s - m).astype(BF16))
        ones = jnp.ones((nk, LANES), BF16)
        rows = pl.ds(r + dil * BLOCK * i, BLOCK, stride=dil) if dil > 1 else pl.ds(i * BLOCK, BLOCK)
        per = LANES // HEAD_DIM
        for c in range(nchunk):
            top = slice(per * c * BLOCK, (per * c + 1) * BLOCK)
            bot = slice((per * c + 1) * BLOCK, (per * c + 2) * BLOCK)
            pv = _dot(p[per * c * BLOCK:per * (c + 1) * BLOCK],
                      jnp.concatenate([vv[:, c * LANES:(c + 1) * LANES], ones], axis=1))
            pv, den = pv[:, :LANES], pv[:, LANES:]
            den_c = jnp.where(first_head, den[:BLOCK], den[BLOCK:])
            m_c = jnp.where(first_head, m[top], m[bot])
            og_ref[g, c, rows, :] = jnp.where(first_head, pv[:BLOCK], pv[BLOCK:]) * (1.0 / den_c)
            lse_ref[g, c, rows, :] = m_c + jnp.log(den_c) * LOG2E

    side = 16
    ahead = 1
    for g, ((_, dil), ref) in enumerate(zip(DIL_GROUPS, (d0_ref, d1_ref, d2_ref))):
        nb = (S // dil) // BLOCK
        units = dil * nb
        assert units % side == 0 and side % ahead == 0

        def unit_group(t, _, g=g, ref=ref, dil=dil, nb=nb):
            where = [((side * t + u) // nb, (side * t + u) % nb) for u in range(side)]
            batches = [where[k:k + ahead] for k in range(0, side, ahead)]
            coming = [scores(ref, r, i) for r, i in batches[0]]
            for k, batch in enumerate(batches):
                ss = coming
                if k + 1 < len(batches):
                    coming = [scores(ref, r, i) for r, i in batches[k + 1]]
                for (r, i), s in zip(batch, ss):
                    attend(g, ref, dil, r, i, s)
            return 0

        lax.fori_loop(0, units // side, unit_group, 0)

    chunk = 2 * BLOCK

    def merge(t, _):
        rows = pl.ds(t * chunk, chunk)
        for c in range(nchunk):
            lses = [lse_ref[g, c, rows, :] for g in range(N_GROUPS)]
            top = functools.reduce(jnp.maximum, lses)
            ws = [jnp.exp2(l - top) for l in lses]
            num = functools.reduce(lambda a, b: a + b, [w * og_ref[g, c, rows, :] for g, w in enumerate(ws)])
            den = functools.reduce(lambda a, b: a + b, ws)
            o_ref[0, rows, c * LANES:(c + 1) * LANES] = (num / den).astype(o_ref.dtype)
        return 0

    lax.fori_loop(0, S // chunk, merge, 0)


def _dilated(dils, S):
    B = dils[0].shape[0]
    return pl.pallas_call(
        _dilated_kernel,
        grid=(B,),
        in_specs=[pl.BlockSpec((1,) + d.shape[1:], lambda b: (b, 0, 0, 0)) for d in dils],
        out_specs=pl.BlockSpec((1, S, DIL_W), lambda b: (b, 0, 0)),
        out_shape=jax.ShapeDtypeStruct((B, S, DIL_W), BF16),
        scratch_shapes=[pltpu.VMEM((2, BLOCK, 2 * BLOCK), F32)]
                       + [pltpu.VMEM((N_GROUPS, DIL_W // LANES, S, LANES), F32)] * 2,
        compiler_params=_params("parallel"),
        name="dilated",
    )(*dils)


def _mix_kernel(x_ref, oa_ref, ob_ref, qc_ref, kv_ref, g_pre_ref, g_post_ref, w_gate_ref, b_gate_ref,
                w_a_ref, w_b_ref, w_c_ref, w_o_ref, out_ref):
    D = x_ref.shape[-1]
    for rows in _sub_tiles(x_ref.shape[1]):
        x = x_ref[0, rows, :]
        h = _rms_norm(x, g_pre_ref[...]).astype(BF16)

        heads = []
        for hd in range(MEM_HEADS):
            cols = slice(hd * MEM_HEAD_DIM, (hd + 1) * MEM_HEAD_DIM)
            s = _dot_nt(qc_ref[0, rows, cols], kv_ref[0, :, cols])
            p = jnp.exp(s - jnp.max(s, axis=-1, keepdims=True))
            den = jnp.sum(p, axis=-1, keepdims=True)
            v = kv_ref[0, :, MEM_W + hd * MEM_HEAD_DIM:MEM_W + (hd + 1) * MEM_HEAD_DIM]
            heads.append((_dot(p.astype(BF16), v) / den).astype(BF16))
        o_c = jnp.concatenate(heads, axis=-1)

        merged = None
        for br, (o, w_ref) in enumerate(((oa_ref[0, rows, :], w_a_ref), (ob_ref[0, rows, :], w_b_ref),
                                         (o_c, w_c_ref))):
            gcols = slice(br * D, (br + 1) * D)
            gate = jax.nn.sigmoid(_dot(h, w_gate_ref[:, gcols]) + b_gate_ref[:, gcols])
            term = gate * _dot(o, w_ref[...])
            merged = term if merged is None else merged + term
        mix = _dot(merged.astype(BF16), w_o_ref[...])
        out_ref[0, rows, :] = x + _rms_norm(mix, g_post_ref[...])


def _mix(x, o_a, o_b, q_c, kv_m, g_pre, g_post, w_gate, b_gate, w_a, w_b, w_c, w_o):
    B, S, D = x.shape
    tm = TOKEN_TILE
    tile = lambda w: pl.BlockSpec((1, tm, w), lambda b, t: (b, t, 0))
    consts = (g_pre, g_post, w_gate, b_gate, w_a, w_b, w_c, w_o)
    return pl.pallas_call(
        _mix_kernel,
        grid=(B, S // tm),
        in_specs=[tile(D), tile(SB_W), tile(DIL_W), tile(MEM_W),
                  pl.BlockSpec((1,) + kv_m.shape[1:], lambda b, t: (b, 0, 0))]
                 + [_const_spec(c.shape) for c in consts],
        out_specs=tile(D),
        out_shape=jax.ShapeDtypeStruct((B, S, D), F32),
        compiler_params=_params("parallel", "parallel"),
        name="mix",
    )(x, o_a, o_b, q_c, kv_m, *consts)


def _ffn_kernel(x_ref, g_pre_ref, g_post_ref, w_gate_ref, w_up_ref, w_out_ref, out_ref):
    for rows in _sub_tiles(x_ref.shape[0]):
        x = x_ref[rows, :]
        h = _rms_norm(x, g_pre_ref[...]).astype(BF16)
        d_ff = w_out_ref.shape[0]
        step = d_ff // FFN_CHUNKS
        f = None
        for c in range(0, d_ff, step):
            part = jax.nn.silu(_dot(h, w_gate_ref[:, c:c + step])) * _dot(h, w_up_ref[:, c:c + step])
            part = _dot(part.astype(BF16), w_out_ref[c:c + step, :])
            f = part if f is None else f + part
        out_ref[rows, :] = x + _rms_norm(f, g_post_ref[...])


def _ffn(x, g_pre, g_post, w_ff, w_out):
    N, D = x.shape
    d_ff = w_out.shape[0]
    tm = FFN_TILE
    half = lambda c: pl.BlockSpec((D, d_ff), lambda t: (0, c), pipeline_mode=pl.Buffered(1))
    return pl.pallas_call(
        _ffn_kernel,
        grid=(N // tm,),
        in_specs=[pl.BlockSpec((tm, D), lambda t: (t, 0)), _const_spec(g_pre.shape), _const_spec(g_post.shape),
                  half(0), half(1), _const_spec(w_out.shape)],
        out_specs=pl.BlockSpec((tm, D), lambda t: (t, 0)),
        out_shape=jax.ShapeDtypeStruct((N, D), F32),
        compiler_params=_params("parallel"),
        name="ffn",
    )(x, g_pre, g_post, w_ff, w_ff, w_out)


def _rope_tables(S):
    half = HEAD_DIM // 2
    inv_freq = ROPE_THETA ** (-jnp.arange(half, dtype=F32) * 2.0 / HEAD_DIM)
    ang = jnp.arange(S, dtype=F32)[:, None] * inv_freq[None, :]
    reps = LANES // half
    return jnp.tile(jnp.cos(ang), (1, reps)), jnp.tile(jnp.sin(ang), (1, reps))


def _rotary_layout(w):
    D = w.shape[0]
    w = w.reshape(D, DIL_HEADS, 2, HEAD_DIM // 2)
    return w.transpose(0, 2, 1, 3).reshape(D, DIL_W)


def _split_w_in(w):
    scale = HEAD_DIM ** -0.5
    w_sb = jnp.concatenate([w[:, :SB_W] * scale, w[:, SB_W:3 * SB_W]], axis=1).astype(BF16)
    w_dil = []
    off = 3 * SB_W
    for _ in DIL_GROUPS:
        q, k, v = (w[:, off + i * DIL_W:off + (i + 1) * DIL_W] for i in range(3))
        w_dil.append(jnp.concatenate([_rotary_layout(q) * scale, _rotary_layout(k), v], axis=1).astype(BF16))
        off += 3 * DIL_W
    w_qc = w[:, off:off + MEM_W].astype(BF16)
    return w_sb, w_qc, w_dil


def kernel(x, mem, g_pre_mix, g_post_mix, g_pre_ffn, g_post_ffn, g_mem, w_in, w_mem_kv, w_br_sb, w_br_dil,
           w_br_mem, w_gate, b_gate, w_o, w_ffn_in, w_ffn_out):
    B, S, D = x.shape
    depth = w_in.shape[0]
    d_ff = w_ffn_out.shape[1]
    cos, sin = _rope_tables(S)
    row = lambda v: v.reshape(1, -1)
    for l in range(depth):
        w_sb, w_qc, w_dil = _split_w_in(w_in[l])
        kv_m = _memkv(mem, row(g_mem[l]), w_mem_kv[l].astype(BF16))
        sb, q_c, *dils = _inproj(x, row(g_pre_mix[l]), cos, sin, w_sb, w_qc, w_dil)
        o_a = _stick(sb)
        o_b = _dilated(dils, S)
        x = _mix(x, o_a, o_b, q_c, kv_m, row(g_pre_mix[l]), row(g_post_mix[l]), w_gate[l].astype(BF16),
                 row(b_gate[l]), w_br_sb[l].astype(BF16), w_br_dil[l].astype(BF16), w_br_mem[l].astype(BF16),
                 w_o[l].astype(BF16))
        x = _ffn(x.reshape(B * S, D), row(g_pre_ffn[l]), row(g_post_ffn[l]), w_ffn_in[l].astype(BF16),
                 w_ffn_out[l].astype(BF16)).reshape(B, S, D)
    return x
```
